```python
import math
import jax, jax.numpy as jnp
from jax import lax
import numpy as np

D_MODEL = 1024
BATCH = 32
SEQ = 2048
DEPTH = 4

N_MIXERS = 2
CONV_WIDTH = 3
N_HEADS = 16
N_KV_HEADS = 4
HEAD_DIM = D_MODEL // N_HEADS
GROUP = N_HEADS // N_KV_HEADS
WINDOW = 128
BLOCK = WINDOW
ROPE_THETA = 10000.0
D_FF = 2816
EPS = 1e-5
QKV_WIDTH = (N_HEADS + 2 * N_KV_HEADS) * HEAD_DIM
N_CONV_LAYERS = (DEPTH + 1) // 2
N_ATTN_LAYERS = DEPTH // 2

kernel_name = "hybrid_shortconv_swa_sink_convffn"


def rms_norm(x, g):
    xf = x.astype(jnp.float32)
    y = xf * lax.rsqrt(jnp.mean(xf * xf, axis=-1, keepdims=True) + EPS)
    return (y * g.astype(jnp.float32)).astype(x.dtype)


def causal_dwconv(x, w):
    c = x.shape[-1]
    return lax.conv_general_dilated(
        x, w.astype(x.dtype)[:, None, :], window_strides=(1,),
        padding=[(CONV_WIDTH - 1, 0)], dimension_numbers=("NWC", "WIO", "NWC"),
        feature_group_count=c)


def short_conv_mixer(h, w_in, w_conv, w_out):
    bcv = h @ w_in
    b_gate, c_gate, v = jnp.split(bcv, 3, axis=-1)
    y = b_gate * causal_dwconv(c_gate * v, w_conv)
    return y @ w_out


def rope(x, cos, sin):
    x1, x2 = jnp.split(x, 2, axis=-1)
    c = cos[None, :, None, :]
    s = sin[None, :, None, :]
    return jnp.concatenate([x1 * c - x2 * s, x2 * c + x1 * s], axis=-1)


def swa_sink_attention(h, w_qkv, b_qkv, sinks, w_o, b_o, cos, sin):
    bsz, seq, _ = h.shape
    nb = seq // BLOCK
    qkv = h @ w_qkv + b_qkv
    q_w, kv_w = N_HEADS * HEAD_DIM, N_KV_HEADS * HEAD_DIM
    q = qkv[..., :q_w].reshape(bsz, seq, N_HEADS, HEAD_DIM)
    k = qkv[..., q_w:q_w + kv_w].reshape(bsz, seq, N_KV_HEADS, HEAD_DIM)
    v = qkv[..., q_w + kv_w:].reshape(bsz, seq, N_KV_HEADS, HEAD_DIM)
    q = rope(q, cos, sin)
    k = rope(k, cos, sin)

    q = q.reshape(bsz, nb, BLOCK, N_KV_HEADS, GROUP, HEAD_DIM)
    pad = ((0, 0), (1, 0), (0, 0), (0, 0), (0, 0))
    kp = jnp.pad(k.reshape(bsz, nb, BLOCK, N_KV_HEADS, HEAD_DIM), pad)
    vp = jnp.pad(v.reshape(bsz, nb, BLOCK, N_KV_HEADS, HEAD_DIM), pad)
    k_band = jnp.concatenate([kp[:, :-1], kp[:, 1:]], axis=2)
    v_band = jnp.concatenate([vp[:, :-1], vp[:, 1:]], axis=2)

    scores = jnp.einsum("bnqkgd,bnskd->bnkgqs", q, k_band).astype(jnp.float32)
    scores = scores * (HEAD_DIM ** -0.5)

    blk = jnp.arange(nb)[:, None, None]
    qi = jnp.arange(BLOCK)[None, :, None]
    kj = jnp.arange(2 * BLOCK)[None, None, :]
    q_pos = blk * BLOCK + qi
    k_pos = (blk - 1) * BLOCK + kj
    valid = (k_pos <= q_pos) & (q_pos - k_pos < WINDOW) & (k_pos >= 0)
    scores = jnp.where(valid[None, :, None, None], scores, jnp.finfo(jnp.float32).min)

    sink = sinks.astype(jnp.float32).reshape(N_KV_HEADS, GROUP)[None, None, :, :, None, None]
    m = jnp.maximum(jnp.max(scores, axis=-1, keepdims=True), sink)
    p = jnp.exp(scores - m)
    denom = jnp.sum(p, axis=-1, keepdims=True) + jnp.exp(sink - m)
    probs = (p / denom).astype(v_band.dtype)

    o = jnp.einsum("bnkgqs,bnskd->bnqkgd", probs, v_band)
    o = o.reshape(bsz, seq, N_HEADS * HEAD_DIM)
    return o @ w_o + b_o


def conv_ffn(h, w_in, w_conv, w_down):
    gu = h @ w_in
    g, u = jnp.split(gu, 2, axis=-1)
    g = causal_dwconv(g, w_conv)
    return (jax.nn.silu(g) * u) @ w_down


def setup_inputs(seed: int = 0) -> dict:
    key = jax.random.key(seed)
    ks = jax.random.split(key, 16)
    f32 = jnp.float32

    def w(k, shape, fan_in):
        return jax.random.normal(k, shape, f32) * (fan_in ** -0.5)

    return {
        "x": jax.random.normal(ks[0], (BATCH, SEQ, D_MODEL), f32),
        "norm_mix": 1.0 + 0.02 * jax.random.normal(ks[1], (DEPTH, D_MODEL), f32),
        "norm_ffn": 1.0 + 0.02 * jax.random.normal(ks[2], (DEPTH, D_MODEL), f32),
        "norm_final": 1.0 + 0.02 * jax.random.normal(ks[3], (D_MODEL,), f32),
        "conv_w_in": w(ks[4], (N_CONV_LAYERS, D_MODEL, 3 * D_MODEL), D_MODEL),
        "conv_w_conv": w(ks[5], (N_CONV_LAYERS, CONV_WIDTH, D_MODEL), CONV_WIDTH),
        "conv_w_out": w(ks[6], (N_CONV_LAYERS, D_MODEL, D_MODEL), D_MODEL),
        "attn_w_qkv": w(ks[7], (N_ATTN_LAYERS, D_MODEL, QKV_WIDTH), D_MODEL),
        "attn_b_qkv": 0.02 * jax.random.normal(ks[8], (N_ATTN_LAYERS, QKV_WIDTH), f32),
        "attn_sinks": 0.5 * jax.random.normal(ks[9], (N_ATTN_LAYERS, N_HEADS), f32),
        "attn_w_o": w(ks[10], (N_ATTN_LAYERS, N_HEADS * HEAD_DIM, D_MODEL), N_HEADS * HEAD_DIM),
        "attn_b_o": 0.02 * jax.random.normal(ks[11], (N_ATTN_LAYERS, D_MODEL), f32),
        "ffn_w_in": w(ks[12], (DEPTH, D_MODEL, 2 * D_FF), D_MODEL),
        "ffn_w_conv": w(ks[13], (DEPTH, CONV_WIDTH, D_FF), CONV_WIDTH),
        "ffn_w_down": w(ks[14], (DEPTH, D_FF, D_MODEL), D_FF),
    }


def reference(x, norm_mix, norm_ffn, norm_final, conv_w_in, conv_w_conv, conv_w_out,
              attn_w_qkv, attn_b_qkv, attn_sinks, attn_w_o, attn_b_o,
              ffn_w_in, ffn_w_conv, ffn_w_down):
    seq = x.shape[1]
    pos = jnp.arange(seq, dtype=jnp.float32)
    inv_freq = 1.0 / (ROPE_THETA ** (jnp.arange(0, HEAD_DIM, 2, dtype=jnp.float32) / HEAD_DIM))
    ang = pos[:, None] * inv_freq[None, :]
    cos = jnp.cos(ang).astype(x.dtype)
    sin = jnp.sin(ang).astype(x.dtype)

    for i in range(DEPTH):
        h = rms_norm(x, norm_mix[i])
        j = i // N_MIXERS
        if i % N_MIXERS == 0:
            mix = short_conv_mixer(h, conv_w_in[j], conv_w_conv[j], conv_w_out[j])
        else:
            mix = swa_sink_attention(h, attn_w_qkv[j], attn_b_qkv[j], attn_sinks[j],
                                     attn_w_o[j], attn_b_o[j], cos, sin)
        x = x + mix
        x = x + conv_ffn(rms_norm(x, norm_ffn[i]), ffn_w_in[i], ffn_w_conv[i], ffn_w_down[i])
    return rms_norm(x, norm_final)
```

```python
import functools

import jax
import jax.numpy as jnp
from jax import lax
from jax.experimental import pallas as pl
from jax.experimental.pallas import tpu as pltpu

D_MODEL = 1024
DEPTH = 4
N_MIXERS = 2
CONV_WIDTH = 3
N_HEADS = 16
N_KV_HEADS = 4
HEAD_DIM = D_MODEL // N_HEADS
GROUP = N_HEADS // N_KV_HEADS
WINDOW = 128
ROPE_THETA = 10000.0
D_FF = 2816
EPS = 1e-5
Q_WIDTH = N_HEADS * HEAD_DIM
KV_WIDTH = N_KV_HEADS * HEAD_DIM
QKV_WIDTH = Q_WIDTH + 2 * KV_WIDTH

V7X_SUBLANES = 8
V7X_LANES = 128
V7X_MXU_DIM = 256
V7X_VMEM_BYTES = 64 * 1024 * 1024

SEQ_TILE = 512
CHANNEL_CHUNK = V7X_MXU_DIM
VMEM_LIMIT_BYTES = (V7X_VMEM_BYTES * 3) // 4

MASK_VALUE = float(jnp.finfo(jnp.float32).min)


def _rms_norm(x, gain):
    ms = jnp.mean(x * x, axis=-1, keepdims=True)
    return x * lax.rsqrt(ms + EPS) * gain


def _dot(a, b):
    return jnp.dot(a, b, preferred_element_type=jnp.float32)


def _causal_conv3(v, prev, w):
    w0, w1, w2 = w[0:1], w[1:2], w[2:3]
    bulk = w2 * v + w1 * pltpu.roll(v, 1, 0) + w0 * pltpu.roll(v, 2, 0)
    head = v[0:V7X_SUBLANES]
    row = lax.broadcasted_iota(jnp.int32, head.shape, 0)
    h1 = jnp.where(row < 1, pltpu.roll(prev, 1, 0), pltpu.roll(head, 1, 0))
    h2 = jnp.where(row < 2, pltpu.roll(prev, 2, 0), pltpu.roll(head, 2, 0))
    head_y = w2 * head + w1 * h1 + w0 * h2
    return jnp.concatenate([head_y, bulk[V7X_SUBLANES:]], axis=0)


def _conv_mixer_kernel(x_ref, gain_ref, w_in_ref, w_conv_ref, w_out_ref, o_ref,
                       tail_ref, y_ref):
    @pl.when(pl.program_id(1) == 0)
    def _():
        tail_ref[...] = jnp.zeros_like(tail_ref)

    x = x_ref[0]
    h = _rms_norm(x, gain_ref[...]).astype(jnp.bfloat16)
    t = x.shape[0]
    for j in range(D_MODEL // CHANNEL_CHUNK):
        lo, hi = j * CHANNEL_CHUNK, (j + 1) * CHANNEL_CHUNK
        b = _dot(h, w_in_ref[:, lo:hi])
        c = _dot(h, w_in_ref[:, D_MODEL + lo:D_MODEL + hi])
        v = _dot(h, w_in_ref[:, 2 * D_MODEL + lo:2 * D_MODEL + hi])
        cv = c * v
        conv = _causal_conv3(cv, tail_ref[:, lo:hi], w_conv_ref[:, lo:hi])
        tail_ref[:, lo:hi] = cv[t - V7X_SUBLANES:]
        y_ref[:, lo:hi] = (b * conv).astype(jnp.bfloat16)
    o_ref[0] = x + _dot(y_ref[...], w_out_ref[...])


def _conv_ffn_kernel(x_ref, gain_ref, w_in_ref, w_conv_ref, w_down_ref, final_gain_ref,
                     o_ref, tail_ref, act_ref, *, apply_final_norm):
    @pl.when(pl.program_id(1) == 0)
    def _():
        tail_ref[...] = jnp.zeros_like(tail_ref)

    x = x_ref[0]
    h = _rms_norm(x, gain_ref[...]).astype(jnp.bfloat16)
    t = x.shape[0]
    for j in range(D_FF // CHANNEL_CHUNK):
        lo, hi = j * CHANNEL_CHUNK, (j + 1) * CHANNEL_CHUNK
        g = _dot(h, w_in_ref[:, lo:hi])
        u = _dot(h, w_in_ref[:, D_FF + lo:D_FF + hi])
        conv = _causal_conv3(g, tail_ref[:, lo:hi], w_conv_ref[:, lo:hi])
        tail_ref[:, lo:hi] = g[t - V7X_SUBLANES:]
        act_ref[:, lo:hi] = (jax.nn.silu(conv) * u).astype(jnp.bfloat16)
    out = x + _dot(act_ref[...], w_down_ref[...])
    if apply_final_norm:
        out = _rms_norm(out, final_gain_ref[...])
    o_ref[0] = out


def _rope(x, cos, sin_signed, first_half):
    partner = jnp.where(first_half,
                        pltpu.roll(x, V7X_LANES - HEAD_DIM // 2, 1),
                        pltpu.roll(x, HEAD_DIM // 2, 1))
    return x * cos + partner * sin_signed


def _attention_kernel(sinks_ref, x_ref, gain_ref, w_qkv_ref, b_qkv_ref, cos_ref, sin_ref,
                      w_o_ref, b_o_ref, o_ref,
                      k_prev_ref, v_prev_ref, q_ref, k_ref, v_ref, att_ref):
    seq_tile = pl.program_id(1)

    @pl.when(seq_tile == 0)
    def _():
        k_prev_ref[...] = jnp.zeros_like(k_prev_ref)
        v_prev_ref[...] = jnp.zeros_like(v_prev_ref)

    x = x_ref[0]
    t = x.shape[0]
    h = _rms_norm(x, gain_ref[...]).astype(jnp.bfloat16)

    cos = cos_ref[...]
    sin_signed = sin_ref[...]
    lane = lax.broadcasted_iota(jnp.int32, (t, V7X_LANES), 1)
    first_half = (lane % HEAD_DIM) < (HEAD_DIM // 2)
    q_scale = HEAD_DIM ** -0.5
    cos_q, sin_q = cos * q_scale, sin_signed * q_scale

    for c in range(Q_WIDTH // V7X_LANES):
        lo, hi = c * V7X_LANES, (c + 1) * V7X_LANES
        q = _dot(h, w_qkv_ref[:, lo:hi]) + b_qkv_ref[:, lo:hi]
        q_ref[:, lo:hi] = _rope(q, cos_q, sin_q, first_half).astype(jnp.bfloat16)
    for c in range(KV_WIDTH // V7X_LANES):
        lo, hi = c * V7X_LANES, (c + 1) * V7X_LANES
        k = _dot(h, w_qkv_ref[:, Q_WIDTH + lo:Q_WIDTH + hi]) + b_qkv_ref[:, Q_WIDTH + lo:Q_WIDTH + hi]
        k_ref[:, lo:hi] = _rope(k, cos, sin_signed, first_half).astype(jnp.bfloat16)
    v = (_dot(h, w_qkv_ref[:, Q_WIDTH + KV_WIDTH:]) + b_qkv_ref[:, Q_WIDTH + KV_WIDTH:])
    v_ref[...] = v.astype(jnp.bfloat16)

    rows = GROUP * WINDOW
    qi = lax.broadcasted_iota(jnp.int32, (rows, 2 * WINDOW), 0) % WINDOW
    kj = lax.broadcasted_iota(jnp.int32, (rows, 2 * WINDOW), 1)
    valid = (kj > qi) & (kj <= qi + WINDOW)
    valid_first = valid & ((kj >= WINDOW) | (seq_tile > 0))
    head_of_row = lax.broadcasted_iota(jnp.int32, (rows, 1), 0) // WINDOW

    for blk in range(t // WINDOW):
        r0 = blk * WINDOW
        for g in range(N_KV_HEADS):
            c0, c1 = g * HEAD_DIM, (g + 1) * HEAD_DIM
            if blk == 0:
                k_before, v_before = k_prev_ref[:, c0:c1], v_prev_ref[:, c0:c1]
            else:
                k_before = k_ref[r0 - WINDOW:r0, c0:c1]
                v_before = v_ref[r0 - WINDOW:r0, c0:c1]
            k_band = jnp.concatenate([k_before, k_ref[r0:r0 + WINDOW, c0:c1]], axis=0)
            v_band = jnp.concatenate([v_before, v_ref[r0:r0 + WINDOW, c0:c1]], axis=0)
            q_group = jnp.concatenate(
                [q_ref[r0:r0 + WINDOW, (g * GROUP + i) * HEAD_DIM:(g * GROUP + i + 1) * HEAD_DIM]
                 for i in range(GROUP)], axis=0)
            scores = lax.dot_general(q_group, k_band, (((1,), (1,)), ((), ())),
                                     preferred_element_type=jnp.float32)
            scores = jnp.where(valid_first if blk == 0 else valid, scores, MASK_VALUE)
            sink = jnp.zeros((rows, 1), jnp.float32)
            for i in range(GROUP):
                sink = jnp.where(head_of_row == i, sinks_ref[g * GROUP + i], sink)
            m = jnp.maximum(jnp.max(scores, axis=-1, keepdims=True), sink)
            p = jnp.exp(scores - m)
            denom = jnp.sum(p, axis=-1, keepdims=True) + jnp.exp(sink - m)
            o = _dot(p.astype(jnp.bfloat16), v_band) / denom
            for i in range(GROUP):
                head = g * GROUP + i
                att_ref[r0:r0 + WINDOW, head * HEAD_DIM:(head + 1) * HEAD_DIM] = (
                    o[i * WINDOW:(i + 1) * WINDOW].astype(jnp.bfloat16))

    k_prev_ref[...] = k_ref[t - WINDOW:, :]
    v_prev_ref[...] = v_ref[t - WINDOW:, :]
    o_ref[0] = x + _dot(att_ref[...], w_o_ref[...]) + b_o_ref[...]


def _resident(shape):
    return pl.BlockSpec(shape, lambda b, s: (0,) * len(shape), pipeline_mode=pl.Buffered(1))


def _token_spec(width):
    return pl.BlockSpec((1, SEQ_TILE, width), lambda b, s: (b, s, 0))


def _compiler_params():
    return pltpu.CompilerParams(dimension_semantics=("parallel", "arbitrary"),
                                vmem_limit_bytes=VMEM_LIMIT_BYTES)


def _conv_mixer(x, gain, w_in, w_conv, w_out):
    batch, seq, d = x.shape
    return pl.pallas_call(
        _conv_mixer_kernel,
        out_shape=jax.ShapeDtypeStruct(x.shape, x.dtype),
        grid=(batch, seq // SEQ_TILE),
        in_specs=[_token_spec(d), _resident((1, d)), _resident(w_in.shape),
                  _resident(w_conv.shape), _resident(w_out.shape)],
        out_specs=_token_spec(d),
        scratch_shapes=[pltpu.VMEM((V7X_SUBLANES, d), jnp.float32),
                        pltpu.VMEM((SEQ_TILE, d), jnp.bfloat16)],
        compiler_params=_compiler_params(),
        name="conv_mixer",
    )(x, gain, w_in, w_conv, w_out)


def _conv_ffn(x, gain, w_in, w_conv, w_down, final_gain, apply_final_norm):
    batch, seq, d = x.shape
    return pl.pallas_call(
        functools.partial(_conv_ffn_kernel, apply_final_norm=apply_final_norm),
        out_shape=jax.ShapeDtypeStruct(x.shape, x.dtype),
        grid=(batch, seq // SEQ_TILE),
        in_specs=[_token_spec(d), _resident((1, d)), _resident(w_in.shape),
                  _resident(w_conv.shape), _resident(w_down.shape), _resident((1, d))],
        out_specs=_token_spec(d),
        scratch_shapes=[pltpu.VMEM((V7X_SUBLANES, D_FF), jnp.float32),
                        pltpu.VMEM((SEQ_TILE, D_FF), jnp.bfloat16)],
        compiler_params=_compiler_params(),
        name="conv_ffn",
    )(x, gain, w_in, w_conv, w_down, final_gain)


def _attention(x, gain, w_qkv, b_qkv, sinks, w_o, b_o, cos, sin_signed):
    batch, seq, d = x.shape
    rope_spec = pl.BlockSpec((SEQ_TILE, V7X_LANES), lambda b, s: (s, 0))
    return pl.pallas_call(
        _attention_kernel,
        out_shape=jax.ShapeDtypeStruct(x.shape, x.dtype),
        grid=(batch, seq // SEQ_TILE),
        in_specs=[pl.BlockSpec(memory_space=pltpu.SMEM),
                  _token_spec(d), _resident((1, d)), _resident(w_qkv.shape),
                  _resident((1, QKV_WIDTH)), rope_spec, rope_spec,
                  _resident(w_o.shape), _resident((1, d))],
        out_specs=_token_spec(d),
        scratch_shapes=[pltpu.VMEM((WINDOW, KV_WIDTH), jnp.bfloat16),
                        pltpu.VMEM((WINDOW, KV_WIDTH), jnp.bfloat16),
                        pltpu.VMEM((SEQ_TILE, Q_WIDTH), jnp.bfloat16),
                        pltpu.VMEM((SEQ_TILE, KV_WIDTH), jnp.bfloat16),
                        pltpu.VMEM((SEQ_TILE, KV_WIDTH), jnp.bfloat16),
                        pltpu.VMEM((SEQ_TILE, Q_WIDTH), jnp.bfloat16)],
        compiler_params=_compiler_params(),
        name="swa_attention",
    )(sinks, x, gain, w_qkv, b_qkv, cos, sin_signed, w_o, b_o)


def _rope_tables(seq):
    pos = jnp.arange(seq, dtype=jnp.float32)
    inv_freq = 1.0 / (ROPE_THETA ** (jnp.arange(0, HEAD_DIM, 2, dtype=jnp.float32) / HEAD_DIM))
    ang = pos[:, None] * inv_freq[None, :]
    cos, sin = jnp.cos(ang), jnp.sin(ang)
    reps = V7X_LANES // HEAD_DIM
    cos_t = jnp.tile(jnp.concatenate([cos, cos], axis=-1), (1, reps))
    sin_t = jnp.tile(jnp.concatenate([-sin, sin], axis=-1), (1, reps))
    return cos_t, sin_t


def kernel(x, norm_mix, norm_ffn, norm_final, conv_w_in, conv_w_conv, conv_w_out, attn_w_qkv,
           attn_b_qkv, attn_sinks, attn_w_o, attn_b_o, ffn_w_in, ffn_w_conv, ffn_w_down):
    bf16 = jnp.bfloat16
    cos, sin_signed = _rope_tables(x.shape[1])
    final_gain = norm_final[None, :]
    for i in range(DEPTH):
        j = i // N_MIXERS
        if i % N_MIXERS == 0:
            x = _conv_mixer(x, norm_mix[i][None], conv_w_in[j].astype(bf16), conv_w_conv[j],
                            conv_w_out[j].astype(bf16))
        else:
            x = _attention(x, norm_mix[i][None], attn_w_qkv[j].astype(bf16), attn_b_qkv[j][None],
                           attn_sinks[j], attn_w_o[j].astype(bf16), attn_b_o[j][None],
                           cos, sin_signed)
        x = _conv_ffn(x, norm_ffn[i][None], ffn_w_in[i].astype(bf16), ffn_w_conv[i],
                      ffn_w_down[i].astype(bf16), final_gain, apply_final_norm=(i == DEPTH - 1))
    return x
```

```python
import functools
import math

import jax
import jax.numpy as jnp
from jax import lax
from jax.experimental import pallas as pl
from jax.experimental.pallas import tpu as pltpu

D_MODEL = 1024
DEPTH = 4
N_MIXERS = 2
CONV_WIDTH = 3
N_HEADS = 16
N_KV_HEADS = 4
HEAD_DIM = D_MODEL // N_HEADS
GROUP = N_HEADS // N_KV_HEADS
WINDOW = 128
ROPE_THETA = 10000.0
D_FF = 2816
EPS = 1e-5
Q_WIDTH = N_HEADS * HEAD_DIM
KV_WIDTH = N_KV_HEADS * HEAD_DIM
QKV_WIDTH = Q_WIDTH + 2 * KV_WIDTH

V7X_SUBLANES = 8
V7X_BF16_SUBLANES = 16
V7X_LANES = 128
V7X_MXU_DIM = 256
V7X_VMEM_BYTES = 64 * 1024 * 1024

SEQ_TILE = 512
CHANNEL_CHUNK = V7X_MXU_DIM
VMEM_LIMIT_BYTES = (V7X_VMEM_BYTES * 3) // 4

MASK_VALUE = float(jnp.finfo(jnp.float32).min)
LOG2_E = math.log2(math.e)

BAND = 2 * WINDOW
SCORE_COLS = GROUP * WINDOW
SCORE_SLOTS = 2
HEADS_PER_LANE_TILE = V7X_LANES // HEAD_DIM

_NT_DIMS = (((1,), (1,)), ((), ()))


def _rms_norm(x, gain):
    ms = jnp.mean(x * x, axis=-1, keepdims=True)
    return x * lax.rsqrt(ms + EPS) * gain


def _dot(a, b):
    return jnp.dot(a, b, preferred_element_type=jnp.float32)


def _causal_conv3(v, prev, w):
    w0, w1, w2 = w[0:1], w[1:2], w[2:3]
    bulk = w2 * v + w1 * pltpu.roll(v, 1, 0) + w0 * pltpu.roll(v, 2, 0)
    head = v[0:V7X_SUBLANES]
    row = lax.broadcasted_iota(jnp.int32, head.shape, 0)
    h1 = jnp.where(row < 1, pltpu.roll(prev, 1, 0), pltpu.roll(head, 1, 0))
    h2 = jnp.where(row < 2, pltpu.roll(prev, 2, 0), pltpu.roll(head, 2, 0))
    head_y = w2 * head + w1 * h1 + w0 * h2
    return jnp.concatenate([head_y, bulk[V7X_SUBLANES:]], axis=0)


def _conv_mixer_kernel(x_ref, gain_ref, w_in_ref, w_conv_ref, w_out_ref, o_ref,
                       tail_ref, y_ref):
    @pl.when(pl.program_id(1) == 0)
    def _():
        tail_ref[...] = jnp.zeros_like(tail_ref)

    x = x_ref[0]
    h = _rms_norm(x, gain_ref[...]).astype(jnp.bfloat16)
    t = x.shape[0]
    for j in range(D_MODEL // CHANNEL_CHUNK):
        lo, hi = j * CHANNEL_CHUNK, (j + 1) * CHANNEL_CHUNK
        b = _dot(h, w_in_ref[:, lo:hi])
        c = _dot(h, w_in_ref[:, D_MODEL + lo:D_MODEL + hi])
        v = _dot(h, w_in_ref[:, 2 * D_MODEL + lo:2 * D_MODEL + hi])
        cv = c * v
        conv = _causal_conv3(cv, tail_ref[:, lo:hi], w_conv_ref[:, lo:hi])
        tail_ref[:, lo:hi] = cv[t - V7X_SUBLANES:]
        y_ref[:, lo:hi] = (b * conv).astype(jnp.bfloat16)
    o_ref[0] = x + _dot(y_ref[...], w_out_ref[...])


def _conv_ffn_kernel(x_ref, gain_ref, w_in_ref, w_conv_ref, w_down_ref, final_gain_ref,
                     o_ref, tail_ref, act_ref, *, apply_final_norm):
    @pl.when(pl.program_id(1) == 0)
    def _():
        tail_ref[...] = jnp.zeros_like(tail_ref)

    x = x_ref[0]
    h = _rms_norm(x, gain_ref[...]).astype(jnp.bfloat16)
    t = x.shape[0]
    for j in range(D_FF // CHANNEL_CHUNK):
        lo, hi = j * CHANNEL_CHUNK, (j + 1) * CHANNEL_CHUNK
        g = _dot(h, w_in_ref[:, lo:hi])
        u = _dot(h, w_in_ref[:, D_FF + lo:D_FF + hi])
        conv = _causal_conv3(g, tail_ref[:, lo:hi], w_conv_ref[:, lo:hi])
        tail_ref[:, lo:hi] = g[t - V7X_SUBLANES:]
        act_ref[:, lo:hi] = (jax.nn.silu(conv) * u).astype(jnp.bfloat16)
    out = x + _dot(act_ref[...], w_down_ref[...])
    if apply_final_norm:
        out = _rms_norm(out, final_gain_ref[...])
    o_ref[0] = out


def _rope_lanes(x, cos, sin_signed, first_half):
    partner = jnp.where(first_half,
                        pltpu.roll(x, V7X_LANES - HEAD_DIM // 2, 1),
                        pltpu.roll(x, HEAD_DIM // 2, 1))
    return x * cos + partner * sin_signed


def _rope_rows(x, cos, sin_signed):
    half = HEAD_DIM // 2
    partner = jnp.concatenate([x[half:], x[:half]], axis=0)
    return x * cos + partner * sin_signed


def _attention_kernel(sinks_ref, x_ref, gain_ref, w_qv_ref, b_qv_ref, w_k_ref, b_k_ref,
                      cos_ref, sin_ref, cos_q_ref, sin_q_ref, w_o_ref, b_o_ref, o_ref,
                      k_prev_ref, v_prev_ref, q_ref, k_ref, v_ref, att_ref, bias_ref,
                      score_ref, col_max_ref):
    seq_tile = pl.program_id(1)
    bf16 = jnp.bfloat16

    @pl.when(seq_tile == 0)
    def _():
        k_prev_ref[...] = jnp.zeros_like(k_prev_ref)
        v_prev_ref[...] = jnp.zeros_like(v_prev_ref)
        kj = lax.broadcasted_iota(jnp.int32, (BAND, SCORE_COLS), 0)
        col = lax.broadcasted_iota(jnp.int32, (BAND, SCORE_COLS), 1)
        qi = col % WINDOW
        head_of_col = col // WINDOW
        window_bias = jnp.where((kj > qi) & (kj <= qi + WINDOW), 0.0, MASK_VALUE)
        for g in range(N_KV_HEADS):
            sink = jnp.zeros((BAND, SCORE_COLS), jnp.float32)
            for i in range(GROUP):
                sink = jnp.where(head_of_col == i, sinks_ref[g * GROUP + i] * LOG2_E, sink)
            bias_ref[g] = jnp.where(kj == 0, sink, window_bias)

    x = x_ref[0]
    t = x.shape[0]
    lane_tiles = t // V7X_LANES
    h = _rms_norm(x, gain_ref[...]).astype(bf16)

    qv = lax.dot_general(w_qv_ref[...], h, _NT_DIMS, preferred_element_type=jnp.float32)
    cos_q, sin_q = cos_q_ref[...], sin_q_ref[...]
    for head in range(N_HEADS):
        rows = slice(head * HEAD_DIM, (head + 1) * HEAD_DIM)
        bias = jnp.concatenate([b_qv_ref[rows, :]] * lane_tiles, axis=1)
        q_ref[rows, :] = _rope_rows(qv[rows] + bias, cos_q, sin_q).astype(bf16)
    bias = jnp.concatenate([b_qv_ref[Q_WIDTH:, :]] * lane_tiles, axis=1)
    v_ref[...] = (qv[Q_WIDTH:] + bias).astype(bf16)

    k = _dot(h, w_k_ref[...]) + b_k_ref[...]
    cos, sin_signed = cos_ref[...], sin_ref[...]
    lane = lax.broadcasted_iota(jnp.int32, (t, V7X_LANES), 1)
    first_half = (lane % HEAD_DIM) < (HEAD_DIM // 2)
    for c in range(KV_WIDTH // V7X_LANES):
        lanes = slice(c * V7X_LANES, (c + 1) * V7X_LANES)
        k_ref[:, lanes] = _rope_lanes(k[:, lanes], cos, sin_signed, first_half).astype(bf16)

    first_key_row = lax.broadcasted_iota(jnp.int32, (V7X_BF16_SUBLANES, V7X_LANES), 0) == 0
    first_key_lane = lax.broadcasted_iota(jnp.int32, (HEAD_DIM, V7X_LANES), 1) == 0
    kj = lax.broadcasted_iota(jnp.int32, (BAND, SCORE_COLS), 0)
    before_start = (kj > 0) & (kj < WINDOW) & (seq_tile == 0)
    zero_half = jnp.zeros((HEAD_DIM, SCORE_COLS), bf16)
    ones_rows = jnp.ones((V7X_BF16_SUBLANES, BAND), bf16)

    def score_stage(blk, g, slot):
        r0 = blk * WINDOW
        lanes = slice((g // HEADS_PER_LANE_TILE) * V7X_LANES,
                      (g // HEADS_PER_LANE_TILE + 1) * V7X_LANES)
        before = k_prev_ref[:, lanes] if blk == 0 else k_ref[r0 - WINDOW:r0, lanes]
        first = jnp.where(first_key_row, 0.0, before[:V7X_BF16_SUBLANES]).astype(bf16)
        k_band = jnp.concatenate(
            [first, before[V7X_BF16_SUBLANES:], k_ref[r0:r0 + WINDOW, lanes]], axis=0)
        q_group = jnp.concatenate(
            [q_ref[(g * GROUP + i) * HEAD_DIM:(g * GROUP + i + 1) * HEAD_DIM, r0:r0 + WINDOW]
             for i in range(GROUP)], axis=1)
        halves = [zero_half] * HEADS_PER_LANE_TILE
        halves[g % HEADS_PER_LANE_TILE] = q_group
        scores = _dot(k_band, jnp.concatenate(halves, axis=0)) + bias_ref[g]
        if blk == 0:
            scores = jnp.where(before_start, MASK_VALUE, scores)
        score_ref[slot] = scores
        col_max_ref[slot] = jnp.broadcast_to(jnp.max(scores, axis=0, keepdims=True),
                                             (V7X_SUBLANES, SCORE_COLS))

    def value_stage(blk, g, slot):
        r0 = blk * WINDOW
        rows = slice(g * HEAD_DIM, (g + 1) * HEAD_DIM)
        before = v_prev_ref[rows, :] if blk == 0 else v_ref[rows, r0 - WINDOW:r0]
        before = jnp.where(first_key_lane, 0.0, before).astype(bf16)
        v_band = jnp.concatenate([before, v_ref[rows, r0:r0 + WINDOW]], axis=1)
        p = jnp.exp2(score_ref[slot] - col_max_ref[slot, 0:1, :]).astype(bf16)
        o_aug = _dot(jnp.concatenate([v_band, ones_rows], axis=0), p)
        o = o_aug[:HEAD_DIM] / o_aug[HEAD_DIM:HEAD_DIM + 1]
        for i in range(GROUP):
            head = g * GROUP + i
            att_ref[head * HEAD_DIM:(head + 1) * HEAD_DIM, r0:r0 + WINDOW] = (
                o[:, i * WINDOW:(i + 1) * WINDOW])

    work = [(blk, g) for blk in range(t // WINDOW) for g in range(N_KV_HEADS)]
    score_stage(*work[0], 0)
    for n, (blk, g) in enumerate(work):
        if n + 1 < len(work):
            score_stage(*work[n + 1], (n + 1) % SCORE_SLOTS)
        value_stage(blk, g, n % SCORE_SLOTS)

    k_prev_ref[...] = k_ref[t - WINDOW:, :]
    v_prev_ref[...] = v_ref[:, t - WINDOW:]
    att = att_ref[...].T.astype(bf16)
    o_ref[0] = x + _dot(att, w_o_ref[...]) + b_o_ref[...]


def _resident(shape):
    return pl.BlockSpec(shape, lambda b, s: (0,) * len(shape), pipeline_mode=pl.Buffered(1))


def _token_spec(width):
    return pl.BlockSpec((1, SEQ_TILE, width), lambda b, s: (b, s, 0))


def _compiler_params():
    return pltpu.CompilerParams(dimension_semantics=("parallel", "arbitrary"),
                                vmem_limit_bytes=VMEM_LIMIT_BYTES)


def _conv_mixer(x, gain, w_in, w_conv, w_out):
    batch, seq, d = x.shape
    return pl.pallas_call(
        _conv_mixer_kernel,
        out_shape=jax.ShapeDtypeStruct(x.shape, x.dtype),
        grid=(batch, seq // SEQ_TILE),
        in_specs=[_token_spec(d), _resident((1, d)), _resident(w_in.shape),
                  _resident(w_conv.shape), _resident(w_out.shape)],
        out_specs=_token_spec(d),
        scratch_shapes=[pltpu.VMEM((V7X_SUBLANES, d), jnp.float32),
                        pltpu.VMEM((SEQ_TILE, d), jnp.bfloat16)],
        compiler_params=_compiler_params(),
        name="conv_mixer",
    )(x, gain, w_in, w_conv, w_out)


def _conv_ffn(x, gain, w_in, w_conv, w_down, final_gain, apply_final_norm):
    batch, seq, d = x.shape
    return pl.pallas_call(
        functools.partial(_conv_ffn_kernel, apply_final_norm=apply_final_norm),
        out_shape=jax.ShapeDtypeStruct(x.shape, x.dtype),
        grid=(batch, seq // SEQ_TILE),
        in_specs=[_token_spec(d), _resident((1, d)), _resident(w_in.shape),
                  _resident(w_conv.shape), _resident(w_down.shape), _resident((1, d))],
        out_specs=_token_spec(d),
        scratch_shapes=[pltpu.VMEM((V7X_SUBLANES, D_FF), jnp.float32),
                        pltpu.VMEM((SEQ_TILE, D_FF), jnp.bfloat16)],
        compiler_params=_compiler_params(),
        name="conv_ffn",
    )(x, gain, w_in, w_conv, w_down, final_gain)


def _attention(x, gain, w_qkv, b_qkv, sinks, w_o, b_o, rope):
    batch, seq, d = x.shape
    bf16 = jnp.bfloat16
    w_qv = jnp.concatenate([w_qkv[:, :Q_WIDTH], w_qkv[:, Q_WIDTH + KV_WIDTH:]], axis=1).T.astype(bf16)
    b_qv = jnp.concatenate([b_qkv[:Q_WIDTH], b_qkv[Q_WIDTH + KV_WIDTH:]])
    b_qv = jnp.broadcast_to(b_qv[:, None], (Q_WIDTH + KV_WIDTH, V7X_LANES))
    w_k = w_qkv[:, Q_WIDTH:Q_WIDTH + KV_WIDTH].astype(bf16)
    b_k = b_qkv[None, Q_WIDTH:Q_WIDTH + KV_WIDTH]
    cos, sin_signed, cos_q, sin_q = rope
    rope_lane_spec = pl.BlockSpec((SEQ_TILE, V7X_LANES), lambda b, s: (s, 0))
    rope_row_spec = pl.BlockSpec((HEAD_DIM, SEQ_TILE), lambda b, s: (0, s))
    return pl.pallas_call(
        _attention_kernel,
        out_shape=jax.ShapeDtypeStruct(x.shape, x.dtype),
        grid=(batch, seq // SEQ_TILE),
        in_specs=[pl.BlockSpec(memory_space=pltpu.SMEM),
                  _token_spec(d), _resident((1, d)), _resident(w_qv.shape),
                  _resident(b_qv.shape), _resident(w_k.shape), _resident(b_k.shape),
                  rope_lane_spec, rope_lane_spec, rope_row_spec, rope_row_spec,
                  _resident(w_o.shape), _resident((1, d))],
        out_specs=_token_spec(d),
        scratch_shapes=[pltpu.VMEM((WINDOW, KV_WIDTH), bf16),
                        pltpu.VMEM((KV_WIDTH, WINDOW), bf16),
                        pltpu.VMEM((Q_WIDTH, SEQ_TILE), bf16),
                        pltpu.VMEM((SEQ_TILE, KV_WIDTH), bf16),
                        pltpu.VMEM((KV_WIDTH, SEQ_TILE), bf16),
                        pltpu.VMEM((Q_WIDTH, SEQ_TILE), jnp.float32),
                        pltpu.VMEM((N_KV_HEADS, BAND, SCORE_COLS), jnp.float32),
                        pltpu.VMEM((SCORE_SLOTS, BAND, SCORE_COLS), jnp.float32),
                        pltpu.VMEM((SCORE_SLOTS, V7X_SUBLANES, SCORE_COLS), jnp.float32)],
        compiler_params=_compiler_params(),
        name="swa_attention",
    )(sinks, x, gain, w_qv, b_qv, w_k, b_k, cos, sin_signed, cos_q, sin_q, w_o.astype(bf16), b_o)


def _rope_tables(seq):
    pos = jnp.arange(seq, dtype=jnp.float32)
    inv_freq = 1.0 / (ROPE_THETA ** (jnp.arange(0, HEAD_DIM, 2, dtype=jnp.float32) / HEAD_DIM))
    ang = pos[:, None] * inv_freq[None, :]
    cos, sin = jnp.cos(ang), jnp.sin(ang)
    cos_head = jnp.concatenate([cos, cos], axis=-1)
    sin_head = jnp.concatenate([-sin, sin], axis=-1)
    q_scale = HEAD_DIM ** -0.5 * LOG2_E
    return (jnp.tile(cos_head, (1, HEADS_PER_LANE_TILE)), jnp.tile(sin_head, (1, HEADS_PER_LANE_TILE)),
            cos_head.T * q_scale, sin_head.T * q_scale)


def kernel(x, norm_mix, norm_ffn, norm_final, conv_w_in, conv_w_conv, conv_w_out, attn_w_qkv,
           attn_b_qkv, attn_sinks, attn_w_o, attn_b_o, ffn_w_in, ffn_w_conv, ffn_w_down):
    bf16 = jnp.bfloat16
    rope = _rope_tables(x.shape[1])
    final_gain = norm_final[None, :]
    for i in range(DEPTH):
        j = i // N_MIXERS
        if i % N_MIXERS == 0:
            x = _conv_mixer(x, norm_mix[i][None], conv_w_in[j].astype(bf16), conv_w_conv[j],
                            conv_w_out[j].astype(bf16))
        else:
            x = _attention(x, norm_mix[i][None], attn_w_qkv[j], attn_b_qkv[j], attn_sinks[j],
                           attn_w_o[j], attn_b_o[j][None], rope)
        x = _conv_ffn(x, norm_ffn[i][None], ffn_w_in[i].astype(bf16), ffn_w_conv[i],
                      ffn_w_down[i].astype(bf16), final_gain, apply_final_norm=(i == DEPTH - 1))
    return x
```

```python
import functools
import math

import jax
import jax.numpy as jnp
from jax import lax
from jax.experimental import pallas as pl
from jax.experimental.pallas import tpu as pltpu

D_MODEL = 1024
DEPTH = 4
N_MIXERS = 2
CONV_WIDTH = 3
N_HEADS = 16
N_KV_HEADS = 4
HEAD_DIM = D_MODEL // N_HEADS
GROUP = N_HEADS // N_KV_HEADS
WINDOW = 128
ROPE_THETA = 10000.0
D_FF = 2816
EPS = 1e-5
Q_WIDTH = N_HEADS * HEAD_DIM
KV_WIDTH = N_KV_HEADS * HEAD_DIM
QKV_WIDTH = Q_WIDTH + 2 * KV_WIDTH

V7X_SUBLANES = 8
V7X_BF16_SUBLANES = 16
V7X_LANES = 128
V7X_MXU_DIM = 256
V7X_VMEM_BYTES = 64 * 1024 * 1024

SEQ_TILE = 1024
CHANNEL_CHUNK = V7X_MXU_DIM
CONV_STAGE_SLOTS = 2
NORM_BLOCK = 256
VMEM_LIMIT_BYTES = (V7X_VMEM_BYTES * 3) // 4

MASK_VALUE = float(jnp.finfo(jnp.float32).min)
LOG2_E = math.log2(math.e)

BAND = 2 * WINDOW
SCORE_COLS = GROUP * WINDOW
SCORE_SLOTS = 4
HEADS_PER_LANE_TILE = V7X_LANES // HEAD_DIM

_NT_DIMS = (((1,), (1,)), ((), ()))


def _rms_norm(x, gain):
    ms = jnp.mean(x * x, axis=-1, keepdims=True)
    return x * lax.rsqrt(ms + EPS) * gain


def _dot(a, b):
    return jnp.dot(a, b, preferred_element_type=jnp.float32)


def _normed_blocks(x, gain):
    return [_rms_norm(x[r:r + NORM_BLOCK], gain).astype(jnp.bfloat16)
            for r in range(0, x.shape[0], NORM_BLOCK)]


def _blocked_dot(h_blocks, w):
    return jnp.concatenate([_dot(hb, w) for hb in h_blocks], axis=0)


def _shifted_conv3(v, stage_ref, tail_ref, cols, w):
    t = v.shape[0]
    pad = V7X_SUBLANES
    stage_ref[0:pad, :] = tail_ref[:, cols]
    stage_ref[pad:pad + t, :] = v
    tail_ref[:, cols] = v[t - pad:]
    v1 = stage_ref[pad - 1:pad - 1 + t, :]
    v2 = stage_ref[pad - 2:pad - 2 + t, :]
    return w[2:3] * v + w[1:2] * v1 + w[0:1] * v2


def _conv_mixer_kernel(x_ref, gain_ref, w_in_ref, w_conv_ref, w_out_ref, o_ref,
                       tail_ref, y_ref, stage_ref):
    @pl.when(pl.program_id(1) == 0)
    def _():
        tail_ref[...] = jnp.zeros_like(tail_ref)

    x = x_ref[0]
    h = _rms_norm(x, gain_ref[...]).astype(jnp.bfloat16)
    for j in range(D_MODEL // CHANNEL_CHUNK):
        lo, hi = j * CHANNEL_CHUNK, (j + 1) * CHANNEL_CHUNK
        b = _dot(h, w_in_ref[:, lo:hi])
        c = _dot(h, w_in_ref[:, D_MODEL + lo:D_MODEL + hi])
        v = _dot(h, w_in_ref[:, 2 * D_MODEL + lo:2 * D_MODEL + hi])
        conv = _shifted_conv3(c * v, stage_ref.at[j % CONV_STAGE_SLOTS], tail_ref, slice(lo, hi),
                              w_conv_ref[:, lo:hi])
        y_ref[:, lo:hi] = (b * conv).astype(jnp.bfloat16)
    o_ref[0] = x + _dot(y_ref[...], w_out_ref[...])


def _conv_ffn_kernel(x_ref, gain_ref, w_in_ref, w_conv_ref, w_down_ref, final_gain_ref,
                     o_ref, tail_ref, act_ref, stage_ref, *, apply_final_norm):
    @pl.when(pl.program_id(1) == 0)
    def _():
        tail_ref[...] = jnp.zeros_like(tail_ref)

    x = x_ref[0]
    h_blocks = _normed_blocks(x, gain_ref[...])
    h = jnp.concatenate(h_blocks, axis=0)
    for j in range(D_FF // CHANNEL_CHUNK):
        lo, hi = j * CHANNEL_CHUNK, (j + 1) * CHANNEL_CHUNK
        project = functools.partial(_blocked_dot, h_blocks) if j == 0 else functools.partial(_dot, h)
        g = project(w_in_ref[:, lo:hi])
        u = project(w_in_ref[:, D_FF + lo:D_FF + hi])
        conv = _shifted_conv3(g, stage_ref.at[j % CONV_STAGE_SLOTS], tail_ref, slice(lo, hi),
                              w_conv_ref[:, lo:hi])
        act_ref[:, lo:hi] = (jax.nn.silu(conv) * u).astype(jnp.bfloat16)
    out = x + _dot(act_ref[...], w_down_ref[...])
    if apply_final_norm:
        out = _rms_norm(out, final_gain_ref[...])
    o_ref[0] = out


def _rope_lanes(x, cos, sin_signed, first_half):
    partner = jnp.where(first_half,
                        pltpu.roll(x, V7X_LANES - HEAD_DIM // 2, 1),
                        pltpu.roll(x, HEAD_DIM // 2, 1))
    return x * cos + partner * sin_signed


def _rope_rows(x, cos, sin_signed):
    half = HEAD_DIM // 2
    partner = jnp.concatenate([x[half:], x[:half]], axis=0)
    return x * cos + partner * sin_signed


def _attention_kernel(sinks_ref, x_ref, gain_ref, w_qv_ref, b_qv_ref, w_k_ref, b_k_ref,
                      cos_ref, sin_ref, cos_q_ref, sin_q_ref, w_o_ref, b_o_ref, o_ref,
                      k_prev_ref, v_prev_ref, q_ref, k_ref, v_ref, att_ref, bias_ref,
                      score_ref, col_max_ref):
    seq_tile = pl.program_id(1)
    bf16 = jnp.bfloat16

    @pl.when(seq_tile == 0)
    def _():
        k_prev_ref[...] = jnp.zeros_like(k_prev_ref)
        v_prev_ref[...] = jnp.zeros_like(v_prev_ref)
        kj = lax.broadcasted_iota(jnp.int32, (BAND, SCORE_COLS), 0)
        col = lax.broadcasted_iota(jnp.int32, (BAND, SCORE_COLS), 1)
        qi = col % WINDOW
        head_of_col = col // WINDOW
        window_bias = jnp.where((kj > qi) & (kj <= qi + WINDOW), 0.0, MASK_VALUE)
        for g in range(N_KV_HEADS):
            sink = jnp.zeros((BAND, SCORE_COLS), jnp.float32)
            for i in range(GROUP):
                sink = jnp.where(head_of_col == i, sinks_ref[g * GROUP + i] * LOG2_E, sink)
            bias_ref[g] = jnp.where(kj == 0, sink, window_bias)

    x = x_ref[0]
    t = x.shape[0]
    lane_tiles = t // V7X_LANES
    h = _rms_norm(x, gain_ref[...]).astype(bf16)

    k = _dot(h, w_k_ref[...]) + b_k_ref[...]
    cos, sin_signed = cos_ref[...], sin_ref[...]
    lane = lax.broadcasted_iota(jnp.int32, (t, V7X_LANES), 1)
    first_half = (lane % HEAD_DIM) < (HEAD_DIM // 2)
    for c in range(KV_WIDTH // V7X_LANES):
        lanes = slice(c * V7X_LANES, (c + 1) * V7X_LANES)
        k_ref[:, lanes] = _rope_lanes(k[:, lanes], cos, sin_signed, first_half).astype(bf16)

    cos_q, sin_q = cos_q_ref[...], sin_q_ref[...]
    for g in range(N_KV_HEADS):
        group_rows = slice(g * GROUP * HEAD_DIM, (g + 1) * GROUP * HEAD_DIM)
        q = lax.dot_general(w_qv_ref[group_rows, :], h, _NT_DIMS,
                            preferred_element_type=jnp.float32)
        for i in range(GROUP):
            rows = slice(i * HEAD_DIM, (i + 1) * HEAD_DIM)
            out_rows = slice((g * GROUP + i) * HEAD_DIM, (g * GROUP + i + 1) * HEAD_DIM)
            bias = jnp.concatenate([b_qv_ref[out_rows, :]] * lane_tiles, axis=1)
            q_ref[out_rows, :] = _rope_rows(q[rows] + bias, cos_q, sin_q).astype(bf16)
    v = lax.dot_general(w_qv_ref[Q_WIDTH:, :], h, _NT_DIMS, preferred_element_type=jnp.float32)
    bias = jnp.concatenate([b_qv_ref[Q_WIDTH:, :]] * lane_tiles, axis=1)
    v_ref[...] = (v + bias).astype(bf16)

    first_key_row = lax.broadcasted_iota(jnp.int32, (V7X_BF16_SUBLANES, V7X_LANES), 0) == 0
    first_key_lane = lax.broadcasted_iota(jnp.int32, (HEAD_DIM, V7X_LANES), 1) == 0
    kj = lax.broadcasted_iota(jnp.int32, (BAND, SCORE_COLS), 0)
    before_start = (kj > 0) & (kj < WINDOW) & (seq_tile == 0)
    zero_half = jnp.zeros((HEAD_DIM, SCORE_COLS), bf16)
    ones_rows = jnp.ones((V7X_BF16_SUBLANES, BAND), bf16)

    def score_stage(blk, g, slot):
        r0 = blk * WINDOW
        lanes = slice((g // HEADS_PER_LANE_TILE) * V7X_LANES,
                      (g // HEADS_PER_LANE_TILE + 1) * V7X_LANES)
        before = k_prev_ref[:, lanes] if blk == 0 else k_ref[r0 - WINDOW:r0, lanes]
        first = jnp.where(first_key_row, 0.0, before[:V7X_BF16_SUBLANES]).astype(bf16)
        k_band = jnp.concatenate(
            [first, before[V7X_BF16_SUBLANES:], k_ref[r0:r0 + WINDOW, lanes]], axis=0)
        q_group = jnp.concatenate(
            [q_ref[(g * GROUP + i) * HEAD_DIM:(g * GROUP + i + 1) * HEAD_DIM, r0:r0 + WINDOW]
             for i in range(GROUP)], axis=1)
        halves = [zero_half] * HEADS_PER_LANE_TILE
        halves[g % HEADS_PER_LANE_TILE] = q_group
        scores = _dot(k_band, jnp.concatenate(halves, axis=0)) + bias_ref[g]
        if blk == 0:
            scores = jnp.where(before_start, MASK_VALUE, scores)
        score_ref[slot] = scores
        col_max_ref[slot] = jnp.broadcast_to(jnp.max(scores, axis=0, keepdims=True),
                                             (V7X_SUBLANES, SCORE_COLS))

    def value_stage(blk, g, slot):
        r0 = blk * WINDOW
        rows = slice(g * HEAD_DIM, (g + 1) * HEAD_DIM)
        before = v_prev_ref[rows, :] if blk == 0 else v_ref[rows, r0 - WINDOW:r0]
        before = jnp.where(first_key_lane, 0.0, before).astype(bf16)
        v_band = jnp.concatenate([before, v_ref[rows, r0:r0 + WINDOW]], axis=1)
        p = jnp.exp2(score_ref[slot] - col_max_ref[slot, 0:1, :]).astype(bf16)
        o_aug = _dot(jnp.concatenate([v_band, ones_rows], axis=0), p)
        o = o_aug[:HEAD_DIM] / o_aug[HEAD_DIM:HEAD_DIM + 1]
        for i in range(GROUP):
            head = g * GROUP + i
            att_ref[head * HEAD_DIM:(head + 1) * HEAD_DIM, r0:r0 + WINDOW] = (
                o[:, i * WINDOW:(i + 1) * WINDOW])

    work = [(blk, g) for blk in range(t // WINDOW) for g in range(N_KV_HEADS)]
    ahead = SCORE_SLOTS - 1
    for n in range(min(ahead, len(work))):
        score_stage(*work[n], n % SCORE_SLOTS)
    for n, (blk, g) in enumerate(work):
        if n + ahead < len(work):
            score_stage(*work[n + ahead], (n + ahead) % SCORE_SLOTS)
        value_stage(blk, g, n % SCORE_SLOTS)

    k_prev_ref[...] = k_ref[t - WINDOW:, :]
    v_prev_ref[...] = v_ref[:, t - WINDOW:]
    att = att_ref[...].T.astype(bf16)
    o_ref[0] = x + _dot(att, w_o_ref[...]) + b_o_ref[...]


def _resident(shape):
    return pl.BlockSpec(shape, lambda b, s: (0,) * len(shape), pipeline_mode=pl.Buffered(1))


def _token_spec(width):
    return pl.BlockSpec((1, SEQ_TILE, width), lambda b, s: (b, s, 0))


def _compiler_params():
    return pltpu.CompilerParams(dimension_semantics=("parallel", "arbitrary"),
                                vmem_limit_bytes=VMEM_LIMIT_BYTES)


def _conv_mixer(x, gain, w_in, w_conv, w_out):
    batch, seq, d = x.shape
    return pl.pallas_call(
        _conv_mixer_kernel,
        out_shape=jax.ShapeDtypeStruct(x.shape, x.dtype),
        grid=(batch, seq // SEQ_TILE),
        in_specs=[_token_spec(d), _resident((1, d)), _resident(w_in.shape),
                  _resident(w_conv.shape), _resident(w_out.shape)],
        out_specs=_token_spec(d),
        scratch_shapes=[pltpu.VMEM((V7X_SUBLANES, d), jnp.float32),
                        pltpu.VMEM((SEQ_TILE, d), jnp.bfloat16),
                        pltpu.VMEM((CONV_STAGE_SLOTS, V7X_SUBLANES + SEQ_TILE, CHANNEL_CHUNK),
                                   jnp.float32)],
        compiler_params=_compiler_params(),
        name="conv_mixer",
    )(x, gain, w_in, w_conv, w_out)


def _conv_ffn(x, gain, w_in, w_conv, w_down, final_gain, apply_final_norm):
    batch, seq, d = x.shape
    return pl.pallas_call(
        functools.partial(_conv_ffn_kernel, apply_final_norm=apply_final_norm),
        out_shape=jax.ShapeDtypeStruct(x.shape, x.dtype),
        grid=(batch, seq // SEQ_TILE),
        in_specs=[_token_spec(d), _resident((1, d)), _resident(w_in.shape),
                  _resident(w_conv.shape), _resident(w_down.shape), _resident((1, d))],
        out_specs=_token_spec(d),
        scratch_shapes=[pltpu.VMEM((V7X_SUBLANES, D_FF), jnp.float32),
                        pltpu.VMEM((SEQ_TILE, D_FF), jnp.bfloat16),
                        pltpu.VMEM((CONV_STAGE_SLOTS, V7X_SUBLANES + SEQ_TILE, CHANNEL_CHUNK),
                                   jnp.float32)],
        compiler_params=_compiler_params(),
        name="conv_ffn",
    )(x, gain, w_in, w_conv, w_down, final_gain)


def _attention(x, gain, w_qkv, b_qkv, sinks, w_o, b_o, rope):
    batch, seq, d = x.shape
    bf16 = jnp.bfloat16
    w_qv = jnp.concatenate([w_qkv[:, :Q_WIDTH], w_qkv[:, Q_WIDTH + KV_WIDTH:]], axis=1).T.astype(bf16)
    b_qv = jnp.concatenate([b_qkv[:Q_WIDTH], b_qkv[Q_WIDTH + KV_WIDTH:]])
    b_qv = jnp.broadcast_to(b_qv[:, None], (Q_WIDTH + KV_WIDTH, V7X_LANES))
    w_k = w_qkv[:, Q_WIDTH:Q_WIDTH + KV_WIDTH].astype(bf16)
    b_k = b_qkv[None, Q_WIDTH:Q_WIDTH + KV_WIDTH]
    cos, sin_signed, cos_q, sin_q = rope
    rope_lane_spec = pl.BlockSpec((SEQ_TILE, V7X_LANES), lambda b, s: (s, 0))
    rope_row_spec = pl.BlockSpec((HEAD_DIM, SEQ_TILE), lambda b, s: (0, s))
    return pl.pallas_call(
        _attention_kernel,
        out_shape=jax.ShapeDtypeStruct(x.shape, x.dtype),
        grid=(batch, seq // SEQ_TILE),
        in_specs=[pl.BlockSpec(memory_space=pltpu.SMEM),
                  _token_spec(d), _resident((1, d)), _resident(w_qv.shape),
                  _resident(b_qv.shape), _resident(w_k.shape), _resident(b_k.shape),
                  rope_lane_spec, rope_lane_spec, rope_row_spec, rope_row_spec,
                  _resident(w_o.shape), _resident((1, d))],
        out_specs=_token_spec(d),
        scratch_shapes=[pltpu.VMEM((WINDOW, KV_WIDTH), bf16),
                        pltpu.VMEM((KV_WIDTH, WINDOW), bf16),
                        pltpu.VMEM((Q_WIDTH, SEQ_TILE), bf16),
                        pltpu.VMEM((SEQ_TILE, KV_WIDTH), bf16),
                        pltpu.VMEM((KV_WIDTH, SEQ_TILE), bf16),
                        pltpu.VMEM((Q_WIDTH, SEQ_TILE), jnp.float32),
                        pltpu.VMEM((N_KV_HEADS, BAND, SCORE_COLS), jnp.float32),
                        pltpu.VMEM((SCORE_SLOTS, BAND, SCORE_COLS), jnp.float32),
                        pltpu.VMEM((SCORE_SLOTS, V7X_SUBLANES, SCORE_COLS), jnp.float32)],
        compiler_params=_compiler_params(),
        name="swa_attention",
    )(sinks, x, gain, w_qv, b_qv, w_k, b_k, cos, sin_signed, cos_q, sin_q, w_o.astype(bf16), b_o)


def _rope_tables(seq):
    pos = jnp.arange(seq, dtype=jnp.float32)
    inv_freq = 1.0 / (ROPE_THETA ** (jnp.arange(0, HEAD_DIM, 2, dtype=jnp.float32) / HEAD_DIM))
    ang = pos[:, None] * inv_freq[None, :]
    cos, sin = jnp.cos(ang), jnp.sin(ang)
    cos_head = jnp.concatenate([cos, cos], axis=-1)
    sin_head = jnp.concatenate([-sin, sin], axis=-1)
    q_scale = HEAD_DIM ** -0.5 * LOG2_E
    return (jnp.tile(cos_head, (1, HEADS_PER_LANE_TILE)), jnp.tile(sin_head, (1, HEADS_PER_LANE_TILE)),
            cos_head.T * q_scale, sin_head.T * q_scale)


def kernel(x, norm_mix, norm_ffn, norm_final, conv_w_in, conv_w_conv, conv_w_out, attn_w_qkv,
           attn_b_qkv, attn_sinks, attn_w_o, attn_b_o, ffn_w_in, ffn_w_conv, ffn_w_down):
    bf16 = jnp.bfloat16
    rope = _rope_tables(x.shape[1])
    final_gain = norm_final[None, :]
    for i in range(DEPTH):
        j = i // N_MIXERS
        if i % N_MIXERS == 0:
            x = _conv_mixer(x, norm_mix[i][None], conv_w_in[j].astype(bf16), conv_w_conv[j],
                            conv_w_out[j].astype(bf16))
        else:
            x = _attention(x, norm_mix[i][None], attn_w_qkv[j], attn_b_qkv[j], attn_sinks[j],
                           attn_w_o[j], attn_b_o[j][None], rope)
        x = _conv_ffn(x, norm_ffn[i][None], ffn_w_in[i].astype(bf16), ffn_w_conv[i],
                      ffn_w_down[i].astype(bf16), final_gain, apply_final_norm=(i == DEPTH - 1))
    return x
```

```python
import functools
import math

import jax
import jax.numpy as jnp
from jax import lax
from jax.experimental import pallas as pl
from jax.experimental.pallas import tpu as pltpu

D_MODEL = 1024
DEPTH = 4
N_MIXERS = 2
CONV_WIDTH = 3
N_HEADS = 16
N_KV_HEADS = 4
HEAD_DIM = D_MODEL // N_HEADS
GROUP = N_HEADS // N_KV_HEADS
WINDOW = 128
ROPE_THETA = 10000.0
D_FF = 2816
EPS = 1e-5
Q_WIDTH = N_HEADS * HEAD_DIM
KV_WIDTH = N_KV_HEADS * HEAD_DIM
QKV_WIDTH = Q_WIDTH + 2 * KV_WIDTH

V7X_SUBLANES = 8
V7X_BF16_SUBLANES = 16
V7X_LANES = 128
V7X_MXU_DIM = 256
V7X_VMEM_BYTES = 64 * 1024 * 1024

SEQ_TILE = 1024
CHANNEL_CHUNK = V7X_MXU_DIM
CONV_STAGE_SLOTS = 2
FFN_DOWN_CAST_BLOCKS = 16
VMEM_LIMIT_BYTES = (V7X_VMEM_BYTES * 3) // 4

MASK_VALUE = float(jnp.finfo(jnp.float32).min)
LOG2_E = math.log2(math.e)

BAND = 2 * WINDOW
SCORE_COLS = GROUP * WINDOW
SCORE_SLOTS = 4
HEADS_PER_LANE_TILE = V7X_LANES // HEAD_DIM
Q_PROJ_ROWS = 256

_NT_DIMS = (((1,), (1,)), ((), ()))


def _rms_norm(x, gain):
    ms = jnp.mean(x * x, axis=-1, keepdims=True)
    return x * lax.rsqrt(ms + EPS) * gain


def _dot(a, b):
    return jnp.dot(a, b, preferred_element_type=jnp.float32)


def _norm_operands(x, gain):
    inv_rms = lax.rsqrt(jnp.mean(x * x, axis=-1, keepdims=True) + EPS)
    xg = x * gain
    return (xg * inv_rms).astype(jnp.bfloat16), xg.astype(jnp.bfloat16), inv_rms


def _shifted_conv3(v, stage_ref, tail_ref, cols, w):
    t = v.shape[0]
    pad = V7X_SUBLANES
    stage_ref[0:pad, :] = tail_ref[:, cols]
    stage_ref[pad:pad + t, :] = v
    tail_ref[:, cols] = v[t - pad:]
    v1 = stage_ref[pad - 1:pad - 1 + t, :]
    v2 = stage_ref[pad - 2:pad - 2 + t, :]
    return w[2:3] * v + w[1:2] * v1 + w[0:1] * v2


def _cast_ffn_weights(w_in_f32_ref, w_down_f32_ref, w_in_bf16_ref, w_down_bf16_ref):
    w_in_bf16_ref[...] = w_in_f32_ref[...].astype(jnp.bfloat16)
    step = pl.program_id(0) * pl.num_programs(1) + pl.program_id(1)

    @pl.when(step < FFN_DOWN_CAST_BLOCKS)
    def _():
        w_down_bf16_ref[...] = w_down_f32_ref[...].astype(jnp.bfloat16)


def _conv_mixer_kernel(x_ref, gain_ref, w_in_ref, w_conv_ref, w_out_ref,
                       ffn_w_in_f32_ref, ffn_w_down_f32_ref,
                       o_ref, ffn_w_in_ref, ffn_w_down_ref,
                       tail_ref, y_ref, stage_ref):
    @pl.when(pl.program_id(1) == 0)
    def _():
        tail_ref[...] = jnp.zeros_like(tail_ref)

    _cast_ffn_weights(ffn_w_in_f32_ref, ffn_w_down_f32_ref, ffn_w_in_ref, ffn_w_down_ref)

    x = x_ref[0]
    h, xg, inv_rms = _norm_operands(x, gain_ref[...])
    for j in range(D_MODEL // CHANNEL_CHUNK):
        lo, hi = j * CHANNEL_CHUNK, (j + 1) * CHANNEL_CHUNK
        if j == 0:
            project = lambda w: _dot(xg, w) * inv_rms
        else:
            project = functools.partial(_dot, h)
        b = project(w_in_ref[:, lo:hi])
        c = project(w_in_ref[:, D_MODEL + lo:D_MODEL + hi])
        v = project(w_in_ref[:, 2 * D_MODEL + lo:2 * D_MODEL + hi])
        conv = _shifted_conv3(c * v, stage_ref.at[j % CONV_STAGE_SLOTS], tail_ref, slice(lo, hi),
                              w_conv_ref[:, lo:hi])
        y_ref[:, lo:hi] = (b * conv).astype(jnp.bfloat16)
    o_ref[0] = x + _dot(y_ref[...], w_out_ref[...])


def _conv_ffn_kernel(x_ref, gain_ref, w_in_ref, w_conv_ref, w_down_ref, final_gain_ref,
                     o_ref, tail_ref, act_ref, stage_ref, *, apply_final_norm):
    @pl.when(pl.program_id(1) == 0)
    def _():
        tail_ref[...] = jnp.zeros_like(tail_ref)

    x = x_ref[0]
    h, xg, inv_rms = _norm_operands(x, gain_ref[...])
    for j in range(D_FF // CHANNEL_CHUNK):
        lo, hi = j * CHANNEL_CHUNK, (j + 1) * CHANNEL_CHUNK
        if j == 0:
            project = lambda w: _dot(xg, w) * inv_rms
        else:
            project = functools.partial(_dot, h)
        g = project(w_in_ref[:, lo:hi])
        u = project(w_in_ref[:, D_FF + lo:D_FF + hi])
        conv = _shifted_conv3(g, stage_ref.at[j % CONV_STAGE_SLOTS], tail_ref, slice(lo, hi),
                              w_conv_ref[:, lo:hi])
        act_ref[:, lo:hi] = (jax.nn.silu(conv) * u).astype(jnp.bfloat16)
    out = x + _dot(act_ref[...], w_down_ref[...])
    if apply_final_norm:
        out = _rms_norm(out, final_gain_ref[...])
    o_ref[0] = out


def _rope_lanes(x, cos, sin_signed, first_half):
    partner = jnp.where(first_half,
                        pltpu.roll(x, V7X_LANES - HEAD_DIM // 2, 1),
                        pltpu.roll(x, HEAD_DIM // 2, 1))
    return x * cos + partner * sin_signed


def _rope_rows(x, cos, sin_signed):
    half = HEAD_DIM // 2
    partner = jnp.concatenate([x[half:], x[:half]], axis=0)
    return x * cos + partner * sin_signed


def _attention_kernel(x_ref, gain_ref, w_qv_ref, b_qv_ref, w_k_ref, b_k_ref,
                      cos_ref, sin_ref, cos_q_ref, sin_q_ref, bias_ref, w_o_ref, b_o_ref,
                      ffn_w_in_f32_ref, ffn_w_down_f32_ref,
                      o_ref, ffn_w_in_ref, ffn_w_down_ref,
                      k_prev_ref, v_prev_ref, q_ref, k_ref, v_ref, att_ref,
                      score_ref, col_max_ref):
    seq_tile = pl.program_id(1)
    bf16 = jnp.bfloat16
    _cast_ffn_weights(ffn_w_in_f32_ref, ffn_w_down_f32_ref, ffn_w_in_ref, ffn_w_down_ref)

    @pl.when(seq_tile == 0)
    def _():
        k_prev_ref[...] = jnp.zeros_like(k_prev_ref)
        v_prev_ref[...] = jnp.zeros_like(v_prev_ref)

    x = x_ref[0]
    t = x.shape[0]
    lane_tiles = t // V7X_LANES
    h, xg, inv_rms = _norm_operands(x, gain_ref[...])

    k = _dot(xg, w_k_ref[...]) * inv_rms + b_k_ref[...]
    cos, sin_signed = cos_ref[...], sin_ref[...]
    lane = lax.broadcasted_iota(jnp.int32, (t, V7X_LANES), 1)
    first_half = (lane % HEAD_DIM) < (HEAD_DIM // 2)
    for c in range(KV_WIDTH // V7X_LANES):
        lanes = slice(c * V7X_LANES, (c + 1) * V7X_LANES)
        k_ref[:, lanes] = _rope_lanes(k[:, lanes], cos, sin_signed, first_half).astype(bf16)

    cos_q, sin_q = cos_q_ref[...], sin_q_ref[...]
    heads_per_dot = Q_PROJ_ROWS // HEAD_DIM
    for r in range(0, Q_WIDTH, Q_PROJ_ROWS):
        q = lax.dot_general(w_qv_ref[r:r + Q_PROJ_ROWS, :], h, _NT_DIMS,
                            preferred_element_type=jnp.float32)
        for i in range(heads_per_dot):
            rows = slice(i * HEAD_DIM, (i + 1) * HEAD_DIM)
            out_rows = slice(r + i * HEAD_DIM, r + (i + 1) * HEAD_DIM)
            bias = jnp.concatenate([b_qv_ref[out_rows, :]] * lane_tiles, axis=1)
            q_ref[out_rows, :] = _rope_rows(q[rows] + bias, cos_q, sin_q).astype(bf16)
    v = lax.dot_general(w_qv_ref[Q_WIDTH:, :], h, _NT_DIMS, preferred_element_type=jnp.float32)
    bias = jnp.concatenate([b_qv_ref[Q_WIDTH:, :]] * lane_tiles, axis=1)
    v_ref[...] = (v + bias).astype(bf16)

    first_key_row = lax.broadcasted_iota(jnp.int32, (V7X_BF16_SUBLANES, V7X_LANES), 0) == 0
    first_key_lane = lax.broadcasted_iota(jnp.int32, (HEAD_DIM, V7X_LANES), 1) == 0
    kj = lax.broadcasted_iota(jnp.int32, (BAND, SCORE_COLS), 0)
    before_start = (kj > 0) & (kj < WINDOW) & (seq_tile == 0)
    zero_half = jnp.zeros((HEAD_DIM, SCORE_COLS), bf16)
    ones_rows = jnp.ones((V7X_BF16_SUBLANES, BAND), bf16)

    def score_stage(blk, g, slot):
        r0 = blk * WINDOW
        lanes = slice((g // HEADS_PER_LANE_TILE) * V7X_LANES,
                      (g // HEADS_PER_LANE_TILE + 1) * V7X_LANES)
        before = k_prev_ref[:, lanes] if blk == 0 else k_ref[r0 - WINDOW:r0, lanes]
        first = jnp.where(first_key_row, 0.0, before[:V7X_BF16_SUBLANES]).astype(bf16)
        k_band = jnp.concatenate(
            [first, before[V7X_BF16_SUBLANES:], k_ref[r0:r0 + WINDOW, lanes]], axis=0)
        q_group = jnp.concatenate(
            [q_ref[(g * GROUP + i) * HEAD_DIM:(g * GROUP + i + 1) * HEAD_DIM, r0:r0 + WINDOW]
             for i in range(GROUP)], axis=1)
        halves = [zero_half] * HEADS_PER_LANE_TILE
        halves[g % HEADS_PER_LANE_TILE] = q_group
        scores = _dot(k_band, jnp.concatenate(halves, axis=0)) + bias_ref[g]
        if blk == 0:
            scores = jnp.where(before_start, MASK_VALUE, scores)
        score_ref[slot] = scores
        col_max_ref[slot] = jnp.broadcast_to(jnp.max(scores, axis=0, keepdims=True),
                                             (V7X_SUBLANES, SCORE_COLS))

    def value_stage(blk, g, slot):
        r0 = blk * WINDOW
        rows = slice(g * HEAD_DIM, (g + 1) * HEAD_DIM)
        before = v_prev_ref[rows, :] if blk == 0 else v_ref[rows, r0 - WINDOW:r0]
        before = jnp.where(first_key_lane, 0.0, before).astype(bf16)
        v_band = jnp.concatenate([before, v_ref[rows, r0:r0 + WINDOW]], axis=1)
        p = jnp.exp2(score_ref[slot] - col_max_ref[slot, 0:1, :]).astype(bf16)
        o_aug = _dot(jnp.concatenate([v_band, ones_rows], axis=0), p)
        o = o_aug[:HEAD_DIM] / o_aug[HEAD_DIM:HEAD_DIM + 1]
        for i in range(GROUP):
            head = g * GROUP + i
            att_ref[head * HEAD_DIM:(head + 1) * HEAD_DIM, r0:r0 + WINDOW] = (
                o[:, i * WINDOW:(i + 1) * WINDOW])

    work = [(blk, g) for blk in range(t // WINDOW) for g in range(N_KV_HEADS)]
    ahead = SCORE_SLOTS - 1
    for n in range(min(ahead, len(work))):
        score_stage(*work[n], n % SCORE_SLOTS)
    for n, (blk, g) in enumerate(work):
        if n + ahead < len(work):
            score_stage(*work[n + ahead], (n + ahead) % SCORE_SLOTS)
        value_stage(blk, g, n % SCORE_SLOTS)

    k_prev_ref[...] = k_ref[t - WINDOW:, :]
    v_prev_ref[...] = v_ref[:, t - WINDOW:]
    att_out = lax.dot_general(att_ref[...].astype(bf16), w_o_ref[...], (((0,), (0,)), ((), ())),
                              preferred_element_type=jnp.float32)
    o_ref[0] = x + att_out + b_o_ref[...]


def _resident(shape):
    return pl.BlockSpec(shape, lambda b, s: (0,) * len(shape), pipeline_mode=pl.Buffered(1))


def _resident_layer(stacked, layer):
    tail = stacked.shape[1:]
    return pl.BlockSpec((None,) + tail, lambda b, s: (layer,) + (0,) * len(tail),
                        pipeline_mode=pl.Buffered(1))


def _token_spec(width):
    return pl.BlockSpec((1, SEQ_TILE, width), lambda b, s: (b, s, 0))


def _compiler_params():
    return pltpu.CompilerParams(dimension_semantics=("arbitrary", "arbitrary"),
                                vmem_limit_bytes=VMEM_LIMIT_BYTES)


class _FfnWeightCast:
    def __init__(self, layer, batch, seq, w_in_f32, w_down_f32):
        seq_tiles = seq // SEQ_TILE
        steps = batch * seq_tiles
        d, in_width = w_in_f32.shape[1:]
        d_ff, out_width = w_down_f32.shape[1:]
        in_rows, down_rows = d // steps, d_ff // FFN_DOWN_CAST_BLOCKS
        assert in_rows * steps == d and in_rows % V7X_BF16_SUBLANES == 0
        assert down_rows * FFN_DOWN_CAST_BLOCKS == d_ff and down_rows % V7X_BF16_SUBLANES == 0
        assert FFN_DOWN_CAST_BLOCKS <= steps

        def in_block(b, s):
            return b * seq_tiles + s

        def down_block(b, s):
            return jnp.minimum(b * seq_tiles + s, FFN_DOWN_CAST_BLOCKS - 1)

        self.operands = (w_in_f32, w_down_f32)
        self.in_specs = [
            pl.BlockSpec((None, in_rows, in_width), lambda b, s: (layer, in_block(b, s), 0)),
            pl.BlockSpec((None, down_rows, out_width), lambda b, s: (layer, down_block(b, s), 0))]
        self.out_shapes = [jax.ShapeDtypeStruct((d, in_width), jnp.bfloat16),
                           jax.ShapeDtypeStruct((d_ff, out_width), jnp.bfloat16)]
        self.out_specs = [pl.BlockSpec((in_rows, in_width), lambda b, s: (in_block(b, s), 0)),
                          pl.BlockSpec((down_rows, out_width), lambda b, s: (down_block(b, s), 0))]


def _conv_mixer(x, layer, conv_layer, gain, w_in, w_conv, w_out, ffn_cast):
    batch, seq, d = x.shape
    return pl.pallas_call(
        _conv_mixer_kernel,
        out_shape=[jax.ShapeDtypeStruct(x.shape, x.dtype)] + ffn_cast.out_shapes,
        grid=(batch, seq // SEQ_TILE),
        in_specs=[_token_spec(d), _resident_layer(gain, layer), _resident_layer(w_in, conv_layer),
                  _resident_layer(w_conv, conv_layer), _resident_layer(w_out, conv_layer)]
                 + ffn_cast.in_specs,
        out_specs=[_token_spec(d)] + ffn_cast.out_specs,
        scratch_shapes=[pltpu.VMEM((V7X_SUBLANES, d), jnp.float32),
                        pltpu.VMEM((SEQ_TILE, d), jnp.bfloat16),
                        pltpu.VMEM((CONV_STAGE_SLOTS, V7X_SUBLANES + SEQ_TILE, CHANNEL_CHUNK),
                                   jnp.float32)],
        compiler_params=_compiler_params(),
        name="conv_mixer",
    )(x, gain, w_in, w_conv, w_out, *ffn_cast.operands)


def _conv_ffn(x, layer, gain, w_in, w_conv, w_down, final_gain, apply_final_norm):
    batch, seq, d = x.shape
    return pl.pallas_call(
        functools.partial(_conv_ffn_kernel, apply_final_norm=apply_final_norm),
        out_shape=jax.ShapeDtypeStruct(x.shape, x.dtype),
        grid=(batch, seq // SEQ_TILE),
        in_specs=[_token_spec(d), _resident_layer(gain, layer), _resident(w_in.shape),
                  _resident_layer(w_conv, layer), _resident(w_down.shape),
                  _resident((1, d))],
        out_specs=_token_spec(d),
        scratch_shapes=[pltpu.VMEM((V7X_SUBLANES, D_FF), jnp.float32),
                        pltpu.VMEM((SEQ_TILE, D_FF), jnp.bfloat16),
                        pltpu.VMEM((CONV_STAGE_SLOTS, V7X_SUBLANES + SEQ_TILE, CHANNEL_CHUNK),
                                   jnp.float32)],
        compiler_params=_compiler_params(),
        name="conv_ffn",
    )(x, gain, w_in, w_conv, w_down, final_gain)


def _score_bias(sinks):
    kj = jnp.arange(BAND)[:, None]
    col = jnp.arange(SCORE_COLS)[None, :]
    qi = col % WINDOW
    window_bias = jnp.where((kj > qi) & (kj <= qi + WINDOW), 0.0, MASK_VALUE).astype(jnp.float32)
    sink = jnp.repeat(sinks.reshape(-1, N_KV_HEADS, GROUP) * LOG2_E, WINDOW, axis=-1)
    return jnp.where(kj == 0, sink[:, :, None, :], window_bias)


def _attention(x, layer, attn_layer, gain, w_qv, b_qv, w_k, b_k, bias, w_o, b_o, rope, ffn_cast):
    batch, seq, d = x.shape
    bf16 = jnp.bfloat16
    cos, sin_signed, cos_q, sin_q = rope
    rope_lane_spec = pl.BlockSpec((SEQ_TILE, V7X_LANES), lambda b, s: (s, 0))
    rope_row_spec = pl.BlockSpec((HEAD_DIM, SEQ_TILE), lambda b, s: (0, s))
    return pl.pallas_call(
        _attention_kernel,
        out_shape=[jax.ShapeDtypeStruct(x.shape, x.dtype)] + ffn_cast.out_shapes,
        grid=(batch, seq // SEQ_TILE),
        in_specs=[_token_spec(d), _resident_layer(gain, layer), _resident_layer(w_qv, attn_layer),
                  _resident_layer(b_qv, attn_layer), _resident_layer(w_k, attn_layer),
                  _resident_layer(b_k, attn_layer),
                  rope_lane_spec, rope_lane_spec, rope_row_spec, rope_row_spec,
                  _resident_layer(bias, attn_layer), _resident_layer(w_o, attn_layer),
                  _resident_layer(b_o, attn_layer)] + ffn_cast.in_specs,
        out_specs=[_token_spec(d)] + ffn_cast.out_specs,
        scratch_shapes=[pltpu.VMEM((WINDOW, KV_WIDTH), bf16),
                        pltpu.VMEM((KV_WIDTH, WINDOW), bf16),
                        pltpu.VMEM((Q_WIDTH, SEQ_TILE), bf16),
                        pltpu.VMEM((SEQ_TILE, KV_WIDTH), bf16),
                        pltpu.VMEM((KV_WIDTH, SEQ_TILE), bf16),
                        pltpu.VMEM((Q_WIDTH, SEQ_TILE), jnp.float32),
                        pltpu.VMEM((SCORE_SLOTS, BAND, SCORE_COLS), jnp.float32),
                        pltpu.VMEM((SCORE_SLOTS, V7X_SUBLANES, SCORE_COLS), jnp.float32)],
        compiler_params=_compiler_params(),
        name="swa_attention",
    )(x, gain, w_qv, b_qv, w_k, b_k, cos, sin_signed, cos_q, sin_q, bias, w_o, b_o,
      *ffn_cast.operands)


def _rope_tables(seq):
    pos = jnp.arange(seq, dtype=jnp.float32)
    inv_freq = 1.0 / (ROPE_THETA ** (jnp.arange(0, HEAD_DIM, 2, dtype=jnp.float32) / HEAD_DIM))
    ang = pos[:, None] * inv_freq[None, :]
    cos, sin = jnp.cos(ang), jnp.sin(ang)
    cos_head = jnp.concatenate([cos, cos], axis=-1)
    sin_head = jnp.concatenate([-sin, sin], axis=-1)
    q_scale = HEAD_DIM ** -0.5 * LOG2_E
    return (jnp.tile(cos_head, (1, HEADS_PER_LANE_TILE)), jnp.tile(sin_head, (1, HEADS_PER_LANE_TILE)),
            cos_head.T * q_scale, sin_head.T * q_scale)


def kernel(x, norm_mix, norm_ffn, norm_final, conv_w_in, conv_w_conv, conv_w_out, attn_w_qkv,
           attn_b_qkv, attn_sinks, attn_w_o, attn_b_o, ffn_w_in, ffn_w_conv, ffn_w_down):
    bf16 = jnp.bfloat16
    rope = _rope_tables(x.shape[1])
    gain_mix, gain_ffn = norm_mix[:, None, :], norm_ffn[:, None, :]
    final_gain = norm_final[None, :]
    conv_w_in, conv_w_out = conv_w_in.astype(bf16), conv_w_out.astype(bf16)
    v0 = Q_WIDTH + KV_WIDTH
    w_qv = jnp.concatenate([attn_w_qkv[:, :, :Q_WIDTH], attn_w_qkv[:, :, v0:]], axis=2)
    w_qv = jnp.swapaxes(w_qv, 1, 2).astype(bf16)
    b_qv = jnp.concatenate([attn_b_qkv[:, :Q_WIDTH], attn_b_qkv[:, v0:]], axis=1)
    b_qv = jnp.broadcast_to(b_qv[:, :, None], b_qv.shape + (V7X_LANES,))
    w_k = attn_w_qkv[:, :, Q_WIDTH:v0].astype(bf16)
    b_k = attn_b_qkv[:, None, Q_WIDTH:v0]
    bias = _score_bias(attn_sinks)
    w_o, b_o = attn_w_o.astype(bf16), attn_b_o[:, None, :]
    batch, seq, _ = x.shape
    for i in range(DEPTH):
        j = i // N_MIXERS
        ffn_cast = _FfnWeightCast(i, batch, seq, ffn_w_in, ffn_w_down)
        if i % N_MIXERS == 0:
            x, w_in, w_down = _conv_mixer(x, i, j, gain_mix, conv_w_in, conv_w_conv, conv_w_out,
                                          ffn_cast)
        else:
            x, w_in, w_down = _attention(x, i, j, gain_mix, w_qv, b_qv, w_k, b_k, bias, w_o, b_o,
                                         rope, ffn_cast)
        x = _conv_ffn(x, i, gain_ffn, w_in, ffn_w_conv, w_down, final_gain,
                      apply_final_norm=(i == DEPTH - 1))
    return x
```

```python
import functools
import math

import jax
import jax.numpy as jnp
from jax import lax
from jax.experimental import pallas as pl
from jax.experimental.pallas import tpu as pltpu

D_MODEL = 1024
DEPTH = 4
N_MIXERS = 2
CONV_WIDTH = 3
N_HEADS = 16
N_KV_HEADS = 4
HEAD_DIM = D_MODEL // N_HEADS
GROUP = N_HEADS // N_KV_HEADS
WINDOW = 128
ROPE_THETA = 10000.0
D_FF = 2816
EPS = 1e-5
Q_WIDTH = N_HEADS * HEAD_DIM
KV_WIDTH = N_KV_HEADS * HEAD_DIM
QKV_WIDTH = Q_WIDTH + 2 * KV_WIDTH

V7X_SUBLANES = 8
V7X_BF16_SUBLANES = 16
V7X_LANES = 128
V7X_MXU_DIM = 256
V7X_VMEM_BYTES = 64 * 1024 * 1024

SEQ_TILE = 1024
CHANNEL_CHUNK = V7X_MXU_DIM
CONV_STAGE_SLOTS = 2
VMEM_LIMIT_BYTES = (V7X_VMEM_BYTES * 3) // 4

MASK_VALUE = float(jnp.finfo(jnp.float32).min)
LOG2_E = math.log2(math.e)

BAND = 2 * WINDOW
SCORE_COLS = GROUP * WINDOW
SCORE_SLOTS = 4
HEADS_PER_LANE_TILE = V7X_LANES // HEAD_DIM

_NT_DIMS = (((1,), (1,)), ((), ()))
_TN_DIMS = (((0,), (0,)), ((), ()))


def _rms_norm(x, gain):
    ms = jnp.mean(x * x, axis=-1, keepdims=True)
    return x * lax.rsqrt(ms + EPS) * gain


def _dot(a, b):
    return jnp.dot(a, b, preferred_element_type=jnp.float32)


def _norm_operands(x, gain):
    inv_rms = lax.rsqrt(jnp.mean(x * x, axis=-1, keepdims=True) + EPS)
    xg = x * gain
    return (xg * inv_rms).astype(jnp.bfloat16), xg.astype(jnp.bfloat16), inv_rms


def _shifted_conv3(v, stage_ref, tail_ref, cols, w):
    t = v.shape[0]
    pad = V7X_SUBLANES
    stage_ref[0:pad, :] = tail_ref[:, cols]
    stage_ref[pad:pad + t, :] = v
    tail_ref[:, cols] = v[t - pad:]
    v1 = stage_ref[pad - 1:pad - 1 + t, :]
    v2 = stage_ref[pad - 2:pad - 2 + t, :]
    return w[2:3] * v + w[1:2] * v1 + w[0:1] * v2


def _conv_mixer_kernel(x_ref, gain_ref, w_in_ref, w_conv_ref, w_out_ref, o_ref,
                       tail_ref, y_ref, stage_ref):
    @pl.when(pl.program_id(1) == 0)
    def _():
        tail_ref[...] = jnp.zeros_like(tail_ref)

    x = x_ref[0]
    h, xg, inv_rms = _norm_operands(x, gain_ref[...])
    for j in range(D_MODEL // CHANNEL_CHUNK):
        lo, hi = j * CHANNEL_CHUNK, (j + 1) * CHANNEL_CHUNK
        if j == 0:
            project = lambda w: _dot(xg, w) * inv_rms
        else:
            project = functools.partial(_dot, h)
        c = project(w_in_ref[:, D_MODEL + lo:D_MODEL + hi])
        v = project(w_in_ref[:, 2 * D_MODEL + lo:2 * D_MODEL + hi])
        b = project(w_in_ref[:, lo:hi])
        conv = _shifted_conv3(c * v, stage_ref.at[j % CONV_STAGE_SLOTS], tail_ref, slice(lo, hi),
                              w_conv_ref[:, lo:hi])
        y_ref[:, lo:hi] = (b * conv).astype(jnp.bfloat16)
    o_ref[0] = x + _dot(y_ref[...], w_out_ref[...])


def _conv_ffn_kernel(x_ref, gain_ref, w_in_ref, w_conv_ref, w_down_ref, final_gain_ref,
                     o_ref, tail_ref, act_ref, stage_ref, *, apply_final_norm):
    @pl.when(pl.program_id(1) == 0)
    def _():
        tail_ref[...] = jnp.zeros_like(tail_ref)

    x = x_ref[0]
    h, xg, inv_rms = _norm_operands(x, gain_ref[...])
    for j in range(D_FF // CHANNEL_CHUNK):
        lo, hi = j * CHANNEL_CHUNK, (j + 1) * CHANNEL_CHUNK
        if j == 0:
            project = lambda w: _dot(xg, w) * inv_rms
        else:
            project = functools.partial(_dot, h)
        g = project(w_in_ref[:, lo:hi])
        u = project(w_in_ref[:, D_FF + lo:D_FF + hi])
        conv = _shifted_conv3(g, stage_ref.at[j % CONV_STAGE_SLOTS], tail_ref, slice(lo, hi),
                              w_conv_ref[:, lo:hi])
        act_ref[:, lo:hi] = (jax.nn.silu(conv) * u).astype(jnp.bfloat16)
    out = x + _dot(act_ref[...], w_down_ref[...])
    if apply_final_norm:
        out = _rms_norm(out, final_gain_ref[...])
    o_ref[0] = out


def _rope_lanes(x, cos, sin_signed, first_half):
    partner = jnp.where(first_half,
                        pltpu.roll(x, V7X_LANES - HEAD_DIM // 2, 1),
                        pltpu.roll(x, HEAD_DIM // 2, 1))
    return x * cos + partner * sin_signed


def _rope_rows(x, cos, sin_signed):
    half = HEAD_DIM // 2
    partner = jnp.concatenate([x[half:], x[:half]], axis=0)
    return x * cos + partner * sin_signed


def _attention_kernel(x_ref, gain_ref, w_qv_ref, b_qv_ref, w_k_ref, b_k_ref,
                      cos_ref, sin_ref, cos_q_ref, sin_q_ref, bias_ref, w_o_ref, b_o_ref, o_ref,
                      k_prev_ref, v_prev_ref, q_ref, k_ref, v_ref, att_ref,
                      score_ref, col_max_ref):
    seq_tile = pl.program_id(1)
    bf16 = jnp.bfloat16

    @pl.when(seq_tile == 0)
    def _():
        k_prev_ref[...] = jnp.zeros_like(k_prev_ref)
        v_prev_ref[...] = jnp.zeros_like(v_prev_ref)

    x = x_ref[0]
    t = x.shape[0]
    lane_tiles = t // V7X_LANES
    h, xg, inv_rms = _norm_operands(x, gain_ref[...])

    k = _dot(xg, w_k_ref[...]) * inv_rms + b_k_ref[...]
    cos, sin_signed = cos_ref[seq_tile], sin_ref[seq_tile]
    lane = lax.broadcasted_iota(jnp.int32, (t, V7X_LANES), 1)
    first_half = (lane % HEAD_DIM) < (HEAD_DIM // 2)
    for c in range(KV_WIDTH // V7X_LANES):
        lanes = slice(c * V7X_LANES, (c + 1) * V7X_LANES)
        k_ref[:, lanes] = _rope_lanes(k[:, lanes], cos, sin_signed, first_half).astype(bf16)

    cos_q, sin_q = cos_q_ref[seq_tile], sin_q_ref[seq_tile]
    for g in range(N_KV_HEADS):
        group_rows = slice(g * GROUP * HEAD_DIM, (g + 1) * GROUP * HEAD_DIM)
        q = lax.dot_general(w_qv_ref[group_rows, :], h, _NT_DIMS,
                            preferred_element_type=jnp.float32)
        for i in range(GROUP):
            rows = slice(i * HEAD_DIM, (i + 1) * HEAD_DIM)
            out_rows = slice((g * GROUP + i) * HEAD_DIM, (g * GROUP + i + 1) * HEAD_DIM)
            bias = jnp.concatenate([b_qv_ref[out_rows, :]] * lane_tiles, axis=1)
            q_ref[out_rows, :] = _rope_rows(q[rows] + bias, cos_q, sin_q).astype(bf16)
    v = lax.dot_general(w_qv_ref[Q_WIDTH:, :], h, _NT_DIMS, preferred_element_type=jnp.float32)
    bias = jnp.concatenate([b_qv_ref[Q_WIDTH:, :]] * lane_tiles, axis=1)
    v_ref[...] = (v + bias).astype(bf16)

    first_key_row = lax.broadcasted_iota(jnp.int32, (V7X_BF16_SUBLANES, V7X_LANES), 0) == 0
    first_key_lane = lax.broadcasted_iota(jnp.int32, (HEAD_DIM, V7X_LANES), 1) == 0
    kj = lax.broadcasted_iota(jnp.int32, (BAND, SCORE_COLS), 0)
    before_start = (kj > 0) & (kj < WINDOW) & (seq_tile == 0)
    zero_half = jnp.zeros((HEAD_DIM, SCORE_COLS), bf16)
    ones_rows = jnp.ones((V7X_BF16_SUBLANES, BAND), bf16)

    def score_stage(blk, g, slot):
        r0 = blk * WINDOW
        lanes = slice((g // HEADS_PER_LANE_TILE) * V7X_LANES,
                      (g // HEADS_PER_LANE_TILE + 1) * V7X_LANES)
        before = k_prev_ref[:, lanes] if blk == 0 else k_ref[r0 - WINDOW:r0, lanes]
        first = jnp.where(first_key_row, 0.0, before[:V7X_BF16_SUBLANES]).astype(bf16)
        k_band = jnp.concatenate(
            [first, before[V7X_BF16_SUBLANES:], k_ref[r0:r0 + WINDOW, lanes]], axis=0)
        q_group = jnp.concatenate(
            [q_ref[(g * GROUP + i) * HEAD_DIM:(g * GROUP + i + 1) * HEAD_DIM, r0:r0 + WINDOW]
             for i in range(GROUP)], axis=1)
        halves = [zero_half] * HEADS_PER_LANE_TILE
        halves[g % HEADS_PER_LANE_TILE] = q_group
        scores = _dot(k_band, jnp.concatenate(halves, axis=0)) + bias_ref[g]
        if blk == 0:
            scores = jnp.where(before_start, MASK_VALUE, scores)
        score_ref[slot] = scores
        col_max_ref[slot] = jnp.broadcast_to(jnp.max(scores, axis=0, keepdims=True),
                                             (V7X_SUBLANES, SCORE_COLS))

    def value_stage(blk, g, slot):
        r0 = blk * WINDOW
        rows = slice(g * HEAD_DIM, (g + 1) * HEAD_DIM)
        before = v_prev_ref[rows, :] if blk == 0 else v_ref[rows, r0 - WINDOW:r0]
        before = jnp.where(first_key_lane, 0.0, before).astype(bf16)
        v_band = jnp.concatenate([before, v_ref[rows, r0:r0 + WINDOW]], axis=1)
        p = jnp.exp2(score_ref[slot] - col_max_ref[slot, 0:1, :]).astype(bf16)
        o_aug = _dot(jnp.concatenate([v_band, ones_rows], axis=0), p)
        o = o_aug[:HEAD_DIM] / o_aug[HEAD_DIM:HEAD_DIM + 1]
        for i in range(GROUP):
            head = g * GROUP + i
            att_ref[head * HEAD_DIM:(head + 1) * HEAD_DIM, r0:r0 + WINDOW] = (
                o[:, i * WINDOW:(i + 1) * WINDOW])

    work = [(blk, g) for blk in range(t // WINDOW) for g in range(N_KV_HEADS)]
    ahead = SCORE_SLOTS - 1
    for n in range(min(ahead, len(work))):
        score_stage(*work[n], n % SCORE_SLOTS)
    for n, (blk, g) in enumerate(work):
        if n + ahead < len(work):
            score_stage(*work[n + ahead], (n + ahead) % SCORE_SLOTS)
        value_stage(blk, g, n % SCORE_SLOTS)

    k_prev_ref[...] = k_ref[t - WINDOW:, :]
    v_prev_ref[...] = v_ref[:, t - WINDOW:]
    att_out = lax.dot_general(att_ref[...].astype(bf16), w_o_ref[...], _TN_DIMS,
                              preferred_element_type=jnp.float32)
    o_ref[0] = x + att_out + b_o_ref[...]


def _resident(shape):
    return pl.BlockSpec(shape, lambda b, s: (0,) * len(shape), pipeline_mode=pl.Buffered(1))


def _resident_layer(stacked, layer):
    tail = stacked.shape[1:]
    return pl.BlockSpec((None,) + tail, lambda b, s: (layer,) + (0,) * len(tail),
                        pipeline_mode=pl.Buffered(1))


def _token_spec(width):
    return pl.BlockSpec((1, SEQ_TILE, width), lambda b, s: (b, s, 0))


def _compiler_params():
    return pltpu.CompilerParams(dimension_semantics=("parallel", "arbitrary"),
                                vmem_limit_bytes=VMEM_LIMIT_BYTES)


def _conv_mixer(x, layer, conv_layer, gain, w_in, w_conv, w_out):
    batch, seq, d = x.shape
    return pl.pallas_call(
        _conv_mixer_kernel,
        out_shape=jax.ShapeDtypeStruct(x.shape, x.dtype),
        grid=(batch, seq // SEQ_TILE),
        in_specs=[_token_spec(d), _resident_layer(gain, layer), _resident_layer(w_in, conv_layer),
                  _resident_layer(w_conv, conv_layer), _resident_layer(w_out, conv_layer)],
        out_specs=_token_spec(d),
        scratch_shapes=[pltpu.VMEM((V7X_SUBLANES, d), jnp.float32),
                        pltpu.VMEM((SEQ_TILE, d), jnp.bfloat16),
                        pltpu.VMEM((CONV_STAGE_SLOTS, V7X_SUBLANES + SEQ_TILE, CHANNEL_CHUNK),
                                   jnp.float32)],
        compiler_params=_compiler_params(),
        name="conv_mixer",
    )(x, gain, w_in, w_conv, w_out)


def _conv_ffn(x, layer, gain, w_in, w_conv, w_down, final_gain, apply_final_norm):
    batch, seq, d = x.shape
    return pl.pallas_call(
        functools.partial(_conv_ffn_kernel, apply_final_norm=apply_final_norm),
        out_shape=jax.ShapeDtypeStruct(x.shape, x.dtype),
        grid=(batch, seq // SEQ_TILE),
        in_specs=[_token_spec(d), _resident_layer(gain, layer), _resident_layer(w_in, layer),
                  _resident_layer(w_conv, layer), _resident_layer(w_down, layer),
                  _resident((1, d))],
        out_specs=_token_spec(d),
        scratch_shapes=[pltpu.VMEM((V7X_SUBLANES, D_FF), jnp.float32),
                        pltpu.VMEM((SEQ_TILE, D_FF), jnp.bfloat16),
                        pltpu.VMEM((CONV_STAGE_SLOTS, V7X_SUBLANES + SEQ_TILE, CHANNEL_CHUNK),
                                   jnp.float32)],
        compiler_params=_compiler_params(),
        name="conv_ffn",
    )(x, gain, w_in, w_conv, w_down, final_gain)


def _score_bias(sinks):
    kj = jnp.arange(BAND)[:, None]
    col = jnp.arange(SCORE_COLS)[None, :]
    qi = col % WINDOW
    window_bias = jnp.where((kj > qi) & (kj <= qi + WINDOW), 0.0, MASK_VALUE).astype(jnp.float32)
    sink = jnp.repeat(sinks.reshape(-1, N_KV_HEADS, GROUP) * LOG2_E, WINDOW, axis=-1)
    return jnp.where(kj == 0, sink[:, :, None, :], window_bias)


def _attention(x, layer, attn_layer, gain, w_qv, b_qv, w_k, b_k, bias, w_o, b_o, rope):
    batch, seq, d = x.shape
    bf16 = jnp.bfloat16
    cos, sin_signed, cos_q, sin_q = rope
    rope_lane_spec, rope_row_spec = _resident(cos.shape), _resident(cos_q.shape)
    return pl.pallas_call(
        _attention_kernel,
        out_shape=jax.ShapeDtypeStruct(x.shape, x.dtype),
        grid=(batch, seq // SEQ_TILE),
        in_specs=[_token_spec(d), _resident_layer(gain, layer), _resident_layer(w_qv, attn_layer),
                  _resident_layer(b_qv, attn_layer), _resident_layer(w_k, attn_layer),
                  _resident_layer(b_k, attn_layer),
                  rope_lane_spec, rope_lane_spec, rope_row_spec, rope_row_spec,
                  _resident_layer(bias, attn_layer), _resident_layer(w_o, attn_layer),
                  _resident_layer(b_o, attn_layer)],
        out_specs=_token_spec(d),
        scratch_shapes=[pltpu.VMEM((WINDOW, KV_WIDTH), bf16),
                        pltpu.VMEM((KV_WIDTH, WINDOW), bf16),
                        pltpu.VMEM((Q_WIDTH, SEQ_TILE), bf16),
                        pltpu.VMEM((SEQ_TILE, KV_WIDTH), bf16),
                        pltpu.VMEM((KV_WIDTH, SEQ_TILE), bf16),
                        pltpu.VMEM((Q_WIDTH, SEQ_TILE), jnp.float32),
                        pltpu.VMEM((SCORE_SLOTS, BAND, SCORE_COLS), jnp.float32),
                        pltpu.VMEM((SCORE_SLOTS, V7X_SUBLANES, SCORE_COLS), jnp.float32)],
        compiler_params=_compiler_params(),
        name="swa_attention",
    )(x, gain, w_qv, b_qv, w_k, b_k, cos, sin_signed, cos_q, sin_q, bias, w_o, b_o)


def _rope_tables(seq):
    pos = jnp.arange(seq, dtype=jnp.float32)
    inv_freq = 1.0 / (ROPE_THETA ** (jnp.arange(0, HEAD_DIM, 2, dtype=jnp.float32) / HEAD_DIM))
    ang = (pos[:, None] * inv_freq[None, :]).reshape(seq // SEQ_TILE, SEQ_TILE, HEAD_DIM // 2)
    cos, sin = jnp.cos(ang), jnp.sin(ang)
    cos_head = jnp.concatenate([cos, cos], axis=-1)
    sin_head = jnp.concatenate([-sin, sin], axis=-1)
    q_scale = HEAD_DIM ** -0.5 * LOG2_E
    return (jnp.tile(cos_head, (1, 1, HEADS_PER_LANE_TILE)),
            jnp.tile(sin_head, (1, 1, HEADS_PER_LANE_TILE)),
            jnp.swapaxes(cos_head, 1, 2) * q_scale, jnp.swapaxes(sin_head, 1, 2) * q_scale)


def kernel(x, norm_mix, norm_ffn, norm_final, conv_w_in, conv_w_conv, conv_w_out, attn_w_qkv,
           attn_b_qkv, attn_sinks, attn_w_o, attn_b_o, ffn_w_in, ffn_w_conv, ffn_w_down):
    bf16 = jnp.bfloat16
    rope = _rope_tables(x.shape[1])
    gain_mix, gain_ffn = norm_mix[:, None, :], norm_ffn[:, None, :]
    final_gain = norm_final[None, :]
    conv_w_in, conv_w_out = conv_w_in.astype(bf16), conv_w_out.astype(bf16)
    ffn_w_in, ffn_w_down = ffn_w_in.astype(bf16), ffn_w_down.astype(bf16)
    v0 = Q_WIDTH + KV_WIDTH
    w_qv = jnp.concatenate([attn_w_qkv[:, :, :Q_WIDTH], attn_w_qkv[:, :, v0:]], axis=2)
    w_qv = jnp.swapaxes(w_qv, 1, 2).astype(bf16)
    b_qv = jnp.concatenate([attn_b_qkv[:, :Q_WIDTH], attn_b_qkv[:, v0:]], axis=1)
    b_qv = jnp.broadcast_to(b_qv[:, :, None], b_qv.shape + (V7X_LANES,))
    w_k = attn_w_qkv[:, :, Q_WIDTH:v0].astype(bf16)
    b_k = attn_b_qkv[:, None, Q_WIDTH:v0]
    bias = _score_bias(attn_sinks)
    w_o, b_o = attn_w_o.astype(bf16), attn_b_o[:, None, :]
    for i in range(DEPTH):
        j = i // N_MIXERS
        if i % N_MIXERS == 0:
            x = _conv_mixer(x, i, j, gain_mix, conv_w_in, conv_w_conv, conv_w_out)
        else:
            x = _attention(x, i, j, gain_mix, w_qv, b_qv, w_k, b_k, bias, w_o, b_o, rope)
        x = _conv_ffn(x, i, gain_ffn, ffn_w_in, ffn_w_conv, ffn_w_down, final_gain,
                      apply_final_norm=(i == DEPTH - 1))
    return x
```
